```python
import jax, jax.numpy as jnp
from jax import lax
import numpy as np

D_MODEL = 1024
BATCH = 8
SEQ = 2048
DEPTH = 2

D_FF = 2816
NORM_EPS = 1e-6
CONV_WIDTH = 4

LRU_WIDTH = 256
LRU_BLOCKS = 4
LRU_BLOCK = LRU_WIDTH // LRU_BLOCKS
LRU_C = 8.0

RWKV_HEADS = 6
RWKV_HEAD_DIM = 64
RWKV_WIDTH = RWKV_HEADS * RWKV_HEAD_DIM
RWKV_DECAY_LORA = 64
RWKV_AAA_LORA = 64
RWKV_MV_LORA = 32
RWKV_GATE_LORA = 128
RWKV_GN_EPS = 64e-5
RWKV_IN = 3 * RWKV_WIDTH + RWKV_DECAY_LORA + RWKV_AAA_LORA + RWKV_GATE_LORA
RWKV_SPLITS = (RWKV_WIDTH, RWKV_WIDTH, RWKV_WIDTH, RWKV_DECAY_LORA, RWKV_AAA_LORA, RWKV_GATE_LORA)

GDN_HEADS = 6
GDN_HEAD_DIM = 64
GDN_WIDTH = GDN_HEADS * GDN_HEAD_DIM
GDN_CHUNK = 64

D_MIX = LRU_WIDTH + RWKV_WIDTH + GDN_WIDTH
IN_SPLITS = (LRU_WIDTH, LRU_WIDTH, RWKV_IN, 3 * GDN_WIDTH, GDN_WIDTH, GDN_HEADS, GDN_HEADS)
D_IN = 2 * LRU_WIDTH + RWKV_IN + 4 * GDN_WIDTH + 2 * GDN_HEADS

kernel_name = 'hymba_style_rglru_rwkv7_gdn_macaron'


def split_cols(t, sizes):
    idx = np.cumsum(sizes)[:-1].tolist()
    return jnp.split(t, idx, axis=-1)


def rms_norm(x, g, eps=NORM_EPS):
    xf = x.astype(jnp.float32)
    y = xf * lax.rsqrt(jnp.mean(xf * xf, axis=-1, keepdims=True) + eps)
    return (y * g.astype(jnp.float32)).astype(x.dtype)


def l2_normalize(t, eps=1e-6):
    return t * lax.rsqrt(jnp.sum(t * t, axis=-1, keepdims=True) + eps)


def swiglu_ffn(h, wi, wo):
    gate, up = jnp.split(h @ wi, 2, axis=-1)
    return (jax.nn.silu(gate) * up) @ wo


def token_shift(t):
    return jnp.pad(t, ((0, 0), (1, 0), (0, 0)))[:, :-1]


def causal_depthwise_conv(x, w):
    c = x.shape[-1]
    return lax.conv_general_dilated(
        x.astype(jnp.float32), w.astype(jnp.float32)[:, None, :],
        window_strides=(1,), padding=((CONV_WIDTH - 1, 0),),
        dimension_numbers=('NWC', 'WIO', 'NWC'), feature_group_count=c)


def _linear_combine(left, right):
    a1, b1 = left
    a2, b2 = right
    return a1 * a2, a2 * b1 + b2


def rglru_group(p_x, p_y, conv_w, conv_b, ga_w, ga_b, gx_w, gx_b, lam, out_g):
    bsz, T, _ = p_x.shape
    xc = causal_depthwise_conv(p_x, conv_w) + conv_b.astype(jnp.float32)
    xb = xc.reshape(bsz, T, LRU_BLOCKS, LRU_BLOCK)
    r = jax.nn.sigmoid(jnp.einsum('btnc,ncd->btnd', xb, ga_w.astype(jnp.float32)) + ga_b).reshape(bsz, T, LRU_WIDTH)
    i = jax.nn.sigmoid(jnp.einsum('btnc,ncd->btnd', xb, gx_w.astype(jnp.float32)) + gx_b).reshape(bsz, T, LRU_WIDTH)
    log_a = -LRU_C * r * jax.nn.softplus(-lam.astype(jnp.float32))
    a = jnp.exp(log_a)
    mult = jnp.sqrt(-jnp.expm1(2.0 * log_a))
    first = (jnp.arange(T) == 0)[None, :, None]
    mult = jnp.where(first, 1.0, mult)
    _, hseq = lax.associative_scan(_linear_combine, (a, mult * i * xc), axis=1)
    out = hseq * jax.nn.gelu(p_y.astype(jnp.float32))
    return rms_norm(out, out_g)


def rwkv7_group(p, mu, w_up, w_bias, a_up, a_bias, g_up, k_k, k_a, r_k, ln_g, ln_b, v_first, vres):
    p = p.astype(jnp.float32)
    bsz, T, _ = p.shape
    xm = p + (token_shift(p) - p) * mu
    r, k, v, xw, xa, xg = split_cols(xm, RWKV_SPLITS)
    w_log = -jax.nn.softplus(-(w_bias + jnp.tanh(xw) @ w_up)) - 0.5
    decay = jnp.exp(-jnp.exp(w_log))
    a = jax.nn.sigmoid(a_bias + xa @ a_up)
    g = jax.nn.sigmoid(xg) @ g_up
    if vres is None:
        v_first = v
    else:
        vw1, vw2, vb = vres
        v = v + (v_first - v) * jax.nn.sigmoid(vb + (v @ vw1) @ vw2)
    heads = lambda t: t.reshape(bsz, T, RWKV_HEADS, RWKV_HEAD_DIM)
    r, k, v, decay, a = heads(r), heads(k), heads(v), heads(decay), heads(a)
    kk = l2_normalize(k * k_k.reshape(RWKV_HEADS, RWKV_HEAD_DIM))
    k = k * (1.0 + (a - 1.0) * k_a.reshape(RWKV_HEADS, RWKV_HEAD_DIM))

    def step(S, inp):
        r_t, w_t, k_t, v_t, kk_t, b_t = inp
        sa = jnp.einsum('bhvk,bhk->bhv', S, -kk_t)
        S = S * w_t[:, :, None, :] + sa[..., None] * b_t[:, :, None, :] + v_t[..., None] * k_t[:, :, None, :]
        return S, jnp.einsum('bhvk,bhk->bhv', S, r_t)

    tm = lambda t: jnp.swapaxes(t, 0, 1)
    S0 = jnp.zeros((bsz, RWKV_HEADS, RWKV_HEAD_DIM, RWKV_HEAD_DIM), jnp.float32)
    _, y = lax.scan(step, S0, (tm(r), tm(decay), tm(k), tm(v), tm(kk), tm(kk * a)))
    y = tm(y)
    mean = jnp.mean(y, axis=-1, keepdims=True)
    var = jnp.mean(jnp.square(y - mean), axis=-1, keepdims=True)
    y = ((y - mean) * lax.rsqrt(var + RWKV_GN_EPS)).reshape(bsz, T, RWKV_WIDTH) * ln_g + ln_b
    bonus = jnp.sum(r * k * r_k, axis=-1, keepdims=True) * v
    y = (y + bonus.reshape(bsz, T, RWKV_WIDTH)) * g
    return y, v_first


def gated_delta_rule_chunked(q, k, v, g, beta):
    bsz, T, H, D = q.shape
    nc = T // GDN_CHUNK

    def to_chunks(t):
        return t.reshape(bsz, nc, GDN_CHUNK, H, -1).transpose(0, 3, 1, 2, 4)

    q, k, v = to_chunks(q), to_chunks(k), to_chunks(v)
    g = to_chunks(g[..., None])[..., 0]
    beta = to_chunks(beta[..., None])[..., 0]
    gc = jnp.cumsum(g, axis=-1)
    idx = jnp.arange(GDN_CHUNK)
    causal = idx[:, None] >= idx[None, :]
    strict = idx[:, None] > idx[None, :]
    diff = gc[..., :, None] - gc[..., None, :]
    decay = jnp.where(causal, jnp.exp(jnp.where(causal, diff, 0.0)), 0.0)
    k_beta = k * beta[..., None]
    lower = jnp.where(strict, jnp.einsum('bhncd,bhnsd->bhncs', k_beta, k) * decay, 0.0)
    eye = jnp.eye(GDN_CHUNK, dtype=lower.dtype)
    rhs = jnp.concatenate([v * beta[..., None], k_beta * jnp.exp(gc)[..., None]], axis=-1)
    sol = lax.linalg.triangular_solve(lower + eye, rhs, left_side=True, lower=True, unit_diagonal=True)
    u, w = sol[..., :D], sol[..., D:]
    qk = jnp.where(causal, jnp.einsum('bhncd,bhnsd->bhncs', q, k) * decay, 0.0)
    q_dec = q * jnp.exp(gc)[..., None]
    k_dec = k * jnp.exp(gc[..., -1:] - gc)[..., None]
    g_last = jnp.exp(gc[..., -1])

    def step(S, inp):
        q_i, k_i, u_i, w_i, qk_i, gl_i = inp
        v_new = u_i - jnp.einsum('bhcd,bhde->bhce', w_i, S)
        o = jnp.einsum('bhcd,bhde->bhce', q_i, S) + jnp.einsum('bhcs,bhse->bhce', qk_i, v_new)
        S = S * gl_i[..., None, None] + jnp.einsum('bhcd,bhce->bhde', k_i, v_new)
        return S, o

    mv = lambda t: jnp.moveaxis(t, 2, 0)
    S0 = jnp.zeros((bsz, H, D, v.shape[-1]), jnp.float32)
    _, o = lax.scan(step, S0, (mv(q_dec), mv(k_dec), mv(u), mv(w), mv(qk), mv(g_last)))
    return o.transpose(1, 0, 3, 2, 4).reshape(bsz, T, H, -1)


def gdn_group(p_qkv, p_z, p_alpha, p_beta, conv_w, a_log, dt_bias, norm_g):
    bsz, T, _ = p_qkv.shape
    qkv = jax.nn.silu(causal_depthwise_conv(p_qkv, conv_w))
    heads = lambda t: t.reshape(bsz, T, GDN_HEADS, GDN_HEAD_DIM)
    q, k, v = [heads(t) for t in jnp.split(qkv, 3, axis=-1)]
    q = l2_normalize(q) * (GDN_HEAD_DIM ** -0.5)
    k = l2_normalize(k)
    g = -jnp.exp(a_log.astype(jnp.float32)) * jax.nn.softplus(p_alpha.astype(jnp.float32) + dt_bias)
    beta = jax.nn.sigmoid(p_beta.astype(jnp.float32))
    o = gated_delta_rule_chunked(q, k, v, g, beta)
    o = rms_norm(o, norm_g) * jax.nn.silu(heads(p_z.astype(jnp.float32)))
    return o.reshape(bsz, T, GDN_WIDTH)


def setup_inputs(seed: int = 0) -> dict:
    key = jax.random.key(seed)
    keys = jax.random.split(key, 64)
    counter = [0]

    def nk():
        counter[0] += 1
        return keys[counter[0] - 1]

    def nrm(shape, scale):
        return jax.random.normal(nk(), shape, jnp.float32) * scale

    def gain(shape):
        return 1.0 + nrm(shape, 0.02)

    def unif(shape, lo, hi):
        return jax.random.uniform(nk(), shape, jnp.float32, lo, hi)

    L = DEPTH
    x = nrm((BATCH, SEQ, D_MODEL), 1.0)
    ffn1_norm = gain((L, D_MODEL))
    ffn1_wi = nrm((L, D_MODEL, 2 * D_FF), D_MODEL ** -0.5)
    ffn1_wo = nrm((L, D_FF, D_MODEL), D_FF ** -0.5)
    mix_norm = gain((L, D_MODEL))
    w_in = nrm((L, D_MODEL, D_IN), D_MODEL ** -0.5)
    w_out = nrm((L, D_MIX, D_MODEL), D_MIX ** -0.5)
    lru_conv_w = nrm((L, CONV_WIDTH, LRU_WIDTH), 0.5)
    lru_conv_b = nrm((L, LRU_WIDTH), 0.02)
    lru_gate_a_w = nrm((L, LRU_BLOCKS, LRU_BLOCK, LRU_BLOCK), LRU_BLOCK ** -0.5)
    lru_gate_a_b = nrm((L, LRU_BLOCKS, LRU_BLOCK), 0.02)
    lru_gate_x_w = nrm((L, LRU_BLOCKS, LRU_BLOCK, LRU_BLOCK), LRU_BLOCK ** -0.5)
    lru_gate_x_b = nrm((L, LRU_BLOCKS, LRU_BLOCK), 0.02)
    s = unif((L, LRU_WIDTH), 0.9, 0.999) ** (1.0 / LRU_C)
    lru_lambda = jnp.log(s) - jnp.log1p(-s)
    lru_out_norm = gain((L, LRU_WIDTH))
    rwkv_mu = unif((L, RWKV_IN), 0.0, 1.0)
    rwkv_w_up = nrm((L, RWKV_DECAY_LORA, RWKV_WIDTH), 0.1)
    rwkv_w_bias = unif((L, RWKV_WIDTH), -6.5, -1.5)
    rwkv_a_up = nrm((L, RWKV_AAA_LORA, RWKV_WIDTH), 0.5 * RWKV_AAA_LORA ** -0.5)
    rwkv_a_bias = nrm((L, RWKV_WIDTH), 0.1)
    rwkv_g_up = nrm((L, RWKV_GATE_LORA, RWKV_WIDTH), RWKV_GATE_LORA ** -0.5)
    rwkv_k_k = 0.85 + nrm((L, RWKV_WIDTH), 0.02)
    rwkv_k_a = gain((L, RWKV_WIDTH))
    rwkv_r_k = nrm((L, RWKV_HEADS, RWKV_HEAD_DIM), 0.1)
    rwkv_ln_g = gain((L, RWKV_WIDTH))
    rwkv_ln_b = nrm((L, RWKV_WIDTH), 0.02)
    rwkv_vres_w1 = nrm((L - 1, RWKV_WIDTH, RWKV_MV_LORA), RWKV_WIDTH ** -0.5)
    rwkv_vres_w2 = nrm((L - 1, RWKV_MV_LORA, RWKV_WIDTH), 0.5 * RWKV_MV_LORA ** -0.5)
    rwkv_vres_b = 1.0 + nrm((L - 1, RWKV_WIDTH), 0.1)
    gdn_conv_w = nrm((L, CONV_WIDTH, 3 * GDN_WIDTH), 0.5)
    gdn_a_log = jnp.log(unif((L, GDN_HEADS), 1.0, 16.0))
    dt = jnp.exp(unif((L, GDN_HEADS), float(np.log(1e-3)), float(np.log(1e-1))))
    gdn_dt_bias = dt + jnp.log(-jnp.expm1(-dt))
    gdn_norm = gain((L, GDN_HEAD_DIM))
    ffn2_norm = gain((L, D_MODEL))
    ffn2_wi = nrm((L, D_MODEL, 2 * D_FF), D_MODEL ** -0.5)
    ffn2_wo = nrm((L, D_FF, D_MODEL), D_FF ** -0.5)
    final_norm = gain((D_MODEL,))
    return {
        'x': x,
        'ffn1_norm': ffn1_norm, 'ffn1_wi': ffn1_wi, 'ffn1_wo': ffn1_wo,
        'mix_norm': mix_norm, 'w_in': w_in, 'w_out': w_out,
        'lru_conv_w': lru_conv_w, 'lru_conv_b': lru_conv_b,
        'lru_gate_a_w': lru_gate_a_w, 'lru_gate_a_b': lru_gate_a_b,
        'lru_gate_x_w': lru_gate_x_w, 'lru_gate_x_b': lru_gate_x_b,
        'lru_lambda': lru_lambda, 'lru_out_norm': lru_out_norm,
        'rwkv_mu': rwkv_mu, 'rwkv_w_up': rwkv_w_up, 'rwkv_w_bias': rwkv_w_bias,
        'rwkv_a_up': rwkv_a_up, 'rwkv_a_bias': rwkv_a_bias, 'rwkv_g_up': rwkv_g_up,
        'rwkv_k_k': rwkv_k_k, 'rwkv_k_a': rwkv_k_a, 'rwkv_r_k': rwkv_r_k,
        'rwkv_ln_g': rwkv_ln_g, 'rwkv_ln_b': rwkv_ln_b,
        'rwkv_vres_w1': rwkv_vres_w1, 'rwkv_vres_w2': rwkv_vres_w2, 'rwkv_vres_b': rwkv_vres_b,
        'gdn_conv_w': gdn_conv_w, 'gdn_a_log': gdn_a_log, 'gdn_dt_bias': gdn_dt_bias, 'gdn_norm': gdn_norm,
        'ffn2_norm': ffn2_norm, 'ffn2_wi': ffn2_wi, 'ffn2_wo': ffn2_wo,
        'final_norm': final_norm,
    }


def reference(x, ffn1_norm, ffn1_wi, ffn1_wo, mix_norm, w_in, w_out,
              lru_conv_w, lru_conv_b, lru_gate_a_w, lru_gate_a_b, lru_gate_x_w, lru_gate_x_b,
              lru_lambda, lru_out_norm,
              rwkv_mu, rwkv_w_up, rwkv_w_bias, rwkv_a_up, rwkv_a_bias, rwkv_g_up,
              rwkv_k_k, rwkv_k_a, rwkv_r_k, rwkv_ln_g, rwkv_ln_b,
              rwkv_vres_w1, rwkv_vres_w2, rwkv_vres_b,
              gdn_conv_w, gdn_a_log, gdn_dt_bias, gdn_norm,
              ffn2_norm, ffn2_wi, ffn2_wo, final_norm):
    dt = x.dtype
    v_first = None
    for l in range(DEPTH):
        x = x + 0.5 * swiglu_ffn(rms_norm(x, ffn1_norm[l]), ffn1_wi[l], ffn1_wo[l])
        h = rms_norm(x, mix_norm[l])
        p_lx, p_ly, p_rwkv, p_qkv, p_z, p_alpha, p_beta = split_cols(h @ w_in[l], IN_SPLITS)
        y_lru = rglru_group(p_lx, p_ly, lru_conv_w[l], lru_conv_b[l], lru_gate_a_w[l], lru_gate_a_b[l],
                            lru_gate_x_w[l], lru_gate_x_b[l], lru_lambda[l], lru_out_norm[l])
        vres = None if l == 0 else (rwkv_vres_w1[l - 1], rwkv_vres_w2[l - 1], rwkv_vres_b[l - 1])
        y_rwkv, v_first = rwkv7_group(p_rwkv, rwkv_mu[l], rwkv_w_up[l], rwkv_w_bias[l], rwkv_a_up[l],
                                      rwkv_a_bias[l], rwkv_g_up[l], rwkv_k_k[l], rwkv_k_a[l], rwkv_r_k[l],
                                      rwkv_ln_g[l], rwkv_ln_b[l], v_first, vres)
        y_gdn = gdn_group(p_qkv, p_z, p_alpha, p_beta, gdn_conv_w[l], gdn_a_log[l], gdn_dt_bias[l], gdn_norm[l])
        mixed = jnp.concatenate([y_lru, y_rwkv, y_gdn], axis=-1).astype(dt)
        x = x + mixed @ w_out[l]
        x = x + 0.5 * swiglu_ffn(rms_norm(x, ffn2_norm[l]), ffn2_wi[l], ffn2_wo[l])
    return rms_norm(x, final_norm)
```

```python
import functools

import jax
import jax.numpy as jnp
from jax import lax
from jax.experimental import pallas as pl
from jax.experimental.pallas import tpu as pltpu

F32 = jnp.float32
BF16 = jnp.bfloat16

NORM_EPS = 1e-6
L2_EPS = 1e-6
RWKV_GN_EPS = 64e-5
LRU_C = 8.0
HEAD_DIM = 64
N_HEADS = 6
MIX_W = N_HEADS * HEAD_DIM
LRU_W = 256
CHUNK = 64
LOG2_CHUNK = 6
TILE_T = 256
LANES = 128
FFN_TILE = 512
FFN_CHUNK = 256
VMEM_LIMIT = 56 * 1024 * 1024

P_TIME = 2
P_INV = 2
P_APPLY = 2
P_STATE = 2
P_EXACT = 3

_NN = (((1,), (0,)), ((), ()))
_NT = (((1,), (1,)), ((), ()))
_TN = (((0,), (0,)), ((), ()))


def _pieces(x, n):
    if x.dtype == BF16:
        return [x]
    out = []
    r = x
    for i in range(n):
        p = r.astype(BF16)
        out.append(p)
        if i + 1 < n:
            r = r - p.astype(F32)
    return out


def _mm(a, b, dims=_NN, pa=1, pb=1):
    ap = _pieces(a, pa)
    bp = _pieces(b, pb)
    order = max(len(ap), len(bp))
    acc = None
    for i, x in enumerate(ap):
        for j, y in enumerate(bp):
            if i + j < order:
                t = lax.dot_general(x, y, dims, preferred_element_type=F32)
                acc = t if acc is None else acc + t
    return acc


def _sigmoid(x):
    return 1.0 / (1.0 + jnp.exp(-x))


def _softplus(x):
    return jnp.maximum(x, 0.0) + jnp.log1p(jnp.exp(-jnp.abs(x)))


def _expm1(x):
    t = jnp.tanh(0.5 * x)
    return 2.0 * t / (1.0 - t)


def _gelu_tanh(x):
    return 0.5 * x * (1.0 + jnp.tanh(0.7978845608028654 * (x + 0.044715 * (x * x * x))))


def _rms(x, g, eps=NORM_EPS):
    return x * lax.rsqrt(jnp.mean(x * x, axis=-1, keepdims=True) + eps) * g


def _iota(shape, dim):
    return lax.broadcasted_iota(jnp.int32, shape, dim)


def _nil_inverse(lmat, eye, pieces):
    acc = jnp.where(eye, 1.0, lmat)
    power = lmat
    for _ in range(LOG2_CHUNK - 1):
        power = _mm(power, power, pa=pieces, pb=pieces)
        acc = acc + _mm(acc, power, pa=pieces, pb=pieces)
    return acc


def _ffn_body(*refs, d_ff, final):
    if final:
        x_ref, g_ref, wi_ref, wo_ref, gf_ref, o_ref = refs
    else:
        x_ref, g_ref, wi_ref, wo_ref, o_ref = refs
    x = x_ref[...]
    h = _rms(x, g_ref[...]).astype(BF16)
    acc = jnp.zeros_like(x)
    for c in range(d_ff // FFN_CHUNK):
        lo = c * FFN_CHUNK
        gate = _mm(h, wi_ref[:, lo:lo + FFN_CHUNK])
        up = _mm(h, wi_ref[:, d_ff + lo:d_ff + lo + FFN_CHUNK])
        act = (gate * _sigmoid(gate) * up).astype(BF16)
        acc = acc + _mm(act, wo_ref[lo:lo + FFN_CHUNK, :])
    y = x + 0.5 * acc
    if final:
        y = _rms(y, gf_ref[...])
    o_ref[...] = y


def _const_spec(shape):
    return pl.BlockSpec(shape, lambda *_: (0,) * len(shape), pipeline_mode=pl.Buffered(1))


def _ffn_call(x2, g, wi, wo, gf=None):
    n, d = x2.shape
    d_ff = wo.shape[0]
    assert n % FFN_TILE == 0 and d_ff % FFN_CHUNK == 0
    final = gf is not None
    args = [x2, g.reshape(1, d), wi, wo]
    specs = [pl.BlockSpec((FFN_TILE, d), lambda i: (i, 0)), _const_spec((1, d)),
             _const_spec(wi.shape), _const_spec(wo.shape)]
    if final:
        args.append(gf.reshape(1, d))
        specs.append(_const_spec((1, d)))
    return pl.pallas_call(
        functools.partial(_ffn_body, d_ff=d_ff, final=final),
        grid=(n // FFN_TILE,),
        in_specs=specs,
        out_specs=pl.BlockSpec((FFN_TILE, d), lambda i: (i, 0)),
        out_shape=jax.ShapeDtypeStruct((n, d), F32),
        compiler_params=pltpu.CompilerParams(dimension_semantics=("arbitrary",), vmem_limit_bytes=VMEM_LIMIT),
        name="ffn_final" if final else "ffn",
    )(*args)


class _Consts:
    def __init__(self):
        tt = TILE_T
        row = _iota((tt, tt), 0)
        col = _iota((tt, tt), 1)
        same = (row >> LOG2_CHUNK) == (col >> LOG2_CHUNK)
        self.incl = same & (row >= col)
        self.strict = same & (row > col)
        self.eye = row == col
        self.cumsum = jnp.where(self.incl, 1.0, 0.0).astype(BF16)
        self.chunksum = jnp.where(same, 1.0, 0.0).astype(BF16)
        hr = _iota((MIX_W, MIX_W), 0)
        hc = _iota((MIX_W, MIX_W), 1)
        self.headsum = jnp.where((hr >> LOG2_CHUNK) == (hc >> LOG2_CHUNK), 1.0, 0.0).astype(BF16)
        pr = _iota((LANES, LANES), 0)
        pc = _iota((LANES, LANES), 1)
        self.pair_bd = (pr >> LOG2_CHUNK) == (pc >> LOG2_CHUNK)
        self.pair_eye = pr == pc
        self.lane = _iota((1, LANES), 1)
        self.half = (self.lane < HEAD_DIM, self.lane >= HEAD_DIM)
        er = _iota((LANES, MIX_W), 0)
        ec = _iota((LANES, MIX_W), 1) >> LOG2_CHUNK
        self.expand_g = jnp.where(er == ec, 1.0, 0.0).astype(BF16)
        self.expand_b = jnp.where(er == ec + N_HEADS, 1.0, 0.0).astype(BF16)


def _head_sum(x, cst):
    return _mm(x, cst.headsum, pa=P_EXACT)


def _lru(px, py, prm, carry_ref, h_ref, t_idx, y_ref):
    conv_w, conv_b, gate_w, gate_b, lam, out_g = prm
    tt = px.shape[0]
    xx = jnp.concatenate([carry_ref[...], px], axis=0)
    carry_ref[...] = px[tt - 8:, :]
    xc = conv_w[3:4, :] * px + conv_b
    for j in (1, 2, 3):
        xc = xc + conv_w[3 - j:4 - j, :] * pltpu.roll(xx, j, axis=0)[8:, :]
    gates = _mm(xc.astype(BF16), gate_w) + gate_b
    r = _sigmoid(gates[:, :LRU_W])
    i = _sigmoid(gates[:, LRU_W:])
    log_a = -LRU_C * r * _softplus(-lam)
    a = jnp.exp(log_a)
    mult = jnp.sqrt(-_expm1(2.0 * log_a))
    rid = _iota((tt, LRU_W), 0)
    mult = jnp.where((rid == 0) & (t_idx == 0), 1.0, mult)
    b = mult * i * xc
    d = 1
    while d < tt:
        keep = rid >= d
        a_prev = jnp.where(keep, pltpu.roll(a, d, axis=0), 1.0)
        b_prev = jnp.where(keep, pltpu.roll(b, d, axis=0), 0.0)
        b = a * b_prev + b
        a = a * a_prev
        d *= 2
    h = b + a * h_ref[7:8, :]
    h_ref[...] = h[tt - 8:, :]
    y_ref[:, 0:LRU_W] = _rms(h * _gelu_tanh(py), out_g)


def _rwkv(p, prm, vres, carry_ref, s_ref, y_ref, y_off, cst):
    mu, wa_up, w_bias, a_bias, g_up, k_k, k_a, r_k, ln_g, ln_b = prm
    tt = p.shape[0]
    w = MIX_W
    rid = _iota((tt, 1), 0)
    xs = jnp.where(rid == 0, carry_ref[7:8, :], pltpu.roll(p, 1, axis=0))
    carry_ref[...] = p[tt - 8:, :]
    xm = p + (xs - p) * mu
    r = xm[:, 0:w]
    k = xm[:, w:2 * w]
    v = xm[:, 2 * w:3 * w]
    x_lora = xm[:, 3 * w:3 * w + LANES]
    xg = xm[:, 3 * w + LANES:3 * w + 2 * LANES]
    z = jnp.where(cst.half[0], jnp.tanh(x_lora), x_lora)
    lwa = _mm(z.astype(BF16), wa_up)
    w_log = -_softplus(-(w_bias + lwa[:, :w])) - 0.5
    logw = -jnp.exp(w_log)
    a = _sigmoid(a_bias + lwa[:, w:])
    g = _mm(_sigmoid(xg).astype(BF16), g_up)
    v_first = v
    if vres is not None:
        vf, vw1, vw2, vb = vres
        lo = _mm(_mm(v.astype(BF16), vw1).astype(BF16), vw2)
        v = v + (vf - v) * _sigmoid(vb + lo)
    kk = k * k_k
    kk = kk * lax.rsqrt(_head_sum(kk * kk, cst) + L2_EPS)
    k2 = k * (1.0 + (a - 1.0) * k_a)
    bb = kk * a

    cum = _mm(cst.cumsum, logw, pb=P_EXACT)
    tot = _mm(cst.chunksum, logw, pb=P_EXACT)
    inv_w = jnp.exp(-cum)
    rem = jnp.exp(tot - cum)
    wc = jnp.exp(tot)
    r_t = r * jnp.exp(cum)
    a_t = -kk * jnp.exp(cum - logw)
    b_t = bb * inv_w
    k_t = k2 * inv_w
    b_w = bb * rem
    k_w = k2 * rem

    for j in range(N_HEADS // 2):
        sl = slice(j * LANES, (j + 1) * LANES)
        a_p, r_p, b_p, k_p, v_p, bw_p, kw_p = (t[:, sl] for t in (a_t, r_t, b_t, k_t, v, b_w, k_w))
        x2 = jnp.concatenate([b_p, k_p], axis=0)
        per_head = []
        for half in range(2):
            m = cst.half[half]
            x1 = jnp.concatenate([jnp.where(m, a_p, 0.0), jnp.where(m, r_p, 0.0)], axis=0)
            gram = _mm(x1, x2, _NT, pa=P_TIME, pb=P_TIME)
            g_ab = jnp.where(cst.strict, gram[:tt, :tt], 0.0)
            g_ak = jnp.where(cst.strict, gram[:tt, tt:], 0.0)
            m_rb = jnp.where(cst.incl, gram[tt:, :tt], 0.0)
            m_rk = jnp.where(cst.incl, gram[tt:, tt:], 0.0)
            t_inv = _nil_inverse(g_ab, cst.eye, P_INV)
            gv = _mm(jnp.concatenate([g_ak, m_rk], axis=0), v_p, pa=P_APPLY, pb=P_APPLY)
            ut = _mm(t_inv, jnp.concatenate([gv[:tt], a_p], axis=1), pa=P_APPLY, pb=P_APPLY)
            mm = _mm(m_rb, ut, pa=P_APPLY, pb=P_APPLY)
            per_head.append((ut[:, :LANES], ut[:, LANES:], mm[:, :LANES] + gv[tt:], r_p + mm[:, LANES:]))
        u0, ta, y0, ra = (jnp.where(cst.half[0], x0, x1) for x0, x1 in zip(*per_head))
        s = s_ref[j]
        for c in range(tt // CHUNK):
            rc = slice(c * CHUNK, (c + 1) * CHUNK)
            phi = jnp.where(cst.pair_eye, wc[c * CHUNK:c * CHUNK + 1, sl], 0.0) + jnp.where(
                cst.pair_bd, _mm(ta[rc], bw_p[rc], _TN, pa=P_STATE, pb=P_STATE), 0.0)
            delta = jnp.where(cst.pair_bd, _mm(jnp.concatenate([u0[rc], v_p[rc]], axis=0),
                                               jnp.concatenate([bw_p[rc], kw_p[rc]], axis=0),
                                               _TN, pa=P_STATE, pb=P_STATE), 0.0)
            y_ref[rc, y_off + j * LANES:y_off + (j + 1) * LANES] = (
                _mm(ra[rc], s, _NT, pa=P_STATE, pb=P_STATE) + y0[rc])
            s = _mm(s, phi, pa=P_STATE, pb=P_STATE) + delta
        s_ref[j] = s

    y = y_ref[:, y_off:y_off + w]
    mean = _head_sum(y, cst) * (1.0 / HEAD_DIM)
    yc = y - mean
    var = _head_sum(yc * yc, cst) * (1.0 / HEAD_DIM)
    yn = yc * lax.rsqrt(var + RWKV_GN_EPS) * ln_g + ln_b
    bonus = _head_sum(r * k2 * r_k, cst) * v
    y_ref[:, y_off:y_off + w] = (yn + bonus) * g
    return v_first


def _gdn(p_qkv, p_z, p_ab, prm, carry_ref, s_ref, y_ref, y_off, cst):
    conv_w, a_log, dt_bias, norm_g = prm
    tt = p_qkv.shape[0]
    w = MIX_W
    xx = jnp.concatenate([carry_ref[...], p_qkv], axis=0)
    carry_ref[...] = p_qkv[tt - 8:, :]
    conv = conv_w[3:4, :] * p_qkv
    for j in (1, 2, 3):
        conv = conv + conv_w[3 - j:4 - j, :] * pltpu.roll(xx, j, axis=0)[8:, :]
    qkv = conv * _sigmoid(conv)
    q = qkv[:, 0:w]
    k = qkv[:, w:2 * w]
    v = qkv[:, 2 * w:3 * w]
    q = q * lax.rsqrt(_head_sum(q * q, cst) + L2_EPS) * (HEAD_DIM ** -0.5)
    k = k * lax.rsqrt(_head_sum(k * k, cst) + L2_EPS)
    gb = jnp.where(cst.lane < N_HEADS, -jnp.exp(a_log) * _softplus(p_ab + dt_bias), _sigmoid(p_ab))
    beta = _mm(gb, cst.expand_b, pa=P_EXACT)
    gcs = _mm(cst.cumsum, gb, pb=P_EXACT)
    tots = _mm(cst.chunksum, gb, pb=P_EXACT)
    gcs_t = gcs.T
    gc = _mm(gcs, cst.expand_g, pa=P_EXACT)
    tot = _mm(tots, cst.expand_g, pa=P_EXACT)
    egc = jnp.exp(gc)
    kb = k * beta
    vb = v * beta
    kbe = kb * egc
    qd = q * egc
    kd = k * jnp.exp(tot - gc)
    gl = jnp.exp(tot)

    for j in range(N_HEADS // 2):
        sl = slice(j * LANES, (j + 1) * LANES)
        kb_p, q_p, k_p, vb_p, kbe_p, qd_p, kd_p = (t[:, sl] for t in (kb, q, k, vb, kbe, qd, kd))
        rhs = jnp.concatenate([vb_p, kbe_p], axis=1)
        per_head = []
        for half in range(2):
            h = 2 * j + half
            m = cst.half[half]
            diff = gcs[:, h:h + 1] - gcs_t[h:h + 1, :]
            dec = jnp.where(cst.incl, jnp.exp(jnp.where(cst.incl, diff, 0.0)), 0.0)
            x1 = jnp.concatenate([jnp.where(m, kb_p, 0.0), jnp.where(m, q_p, 0.0)], axis=0)
            gram = _mm(x1, k_p, _NT, pa=P_TIME, pb=P_TIME)
            a_mat = jnp.where(cst.strict, gram[:tt] * dec, 0.0)
            qk = gram[tt:] * dec
            t_inv = _nil_inverse(-a_mat, cst.eye, P_INV)
            uw = _mm(t_inv, rhs, pa=P_APPLY, pb=P_APPLY)
            oq = _mm(qk, uw, pa=P_APPLY, pb=P_APPLY)
            per_head.append((uw[:, :LANES], uw[:, LANES:], oq[:, :LANES], qd_p - oq[:, LANES:]))
        u, wm, o0, qa = (jnp.where(cst.half[0], x0, x1) for x0, x1 in zip(*per_head))
        s = s_ref[j]
        for c in range(tt // CHUNK):
            rc = slice(c * CHUNK, (c + 1) * CHUNK)
            ku = _mm(kd_p[rc], jnp.concatenate([u[rc], wm[rc]], axis=1), _TN, pa=P_STATE, pb=P_STATE)
            delta = jnp.where(cst.pair_bd, ku[:, :LANES], 0.0)
            phi = jnp.where(cst.pair_eye, gl[c * CHUNK:c * CHUNK + 1, sl], 0.0) - jnp.where(
                cst.pair_bd, ku[:, LANES:], 0.0)
            y_ref[rc, y_off + j * LANES:y_off + (j + 1) * LANES] = (
                _mm(qa[rc], s, pa=P_STATE, pb=P_STATE) + o0[rc])
            s = _mm(phi, s, pa=P_STATE, pb=P_STATE) + delta
        s_ref[j] = s

    o = y_ref[:, y_off:y_off + w]
    on = o * lax.rsqrt(_head_sum(o * o, cst) * (1.0 / HEAD_DIM) + NORM_EPS) * norm_g
    y_ref[:, y_off:y_off + w] = on * (p_z * _sigmoid(p_z))


_N_LRU, _N_RWKV, _N_VRES, _N_GDN = 6, 10, 3, 4


def _mixer_body(*refs, has_vres):
    it = iter(refs)
    take = lambda n: [next(it) for _ in range(n)]
    x_ref, mixg_ref, win_ref, wout_ref = take(4)
    lru_prm = [r[...] for r in take(_N_LRU)]
    rwkv_prm = [r[...] for r in take(_N_RWKV)]
    vres = None
    if has_vres:
        vf_ref, = take(1)
        vres = [vf_ref[...]] + [r[...] for r in take(_N_VRES)]
    gdn_prm = [r[...] for r in take(_N_GDN)]
    o_ref, = take(1)
    vout_ref = None
    if not has_vres:
        vout_ref, = take(1)
    lru_carry, lru_h, rw_carry, rw_s, gd_carry, gd_s, y_scr = take(7)

    t_idx = pl.program_id(1)

    @pl.when(t_idx == 0)
    def _():
        for ref in (lru_carry, lru_h, rw_carry, rw_s, gd_carry, gd_s):
            ref[...] = jnp.zeros(ref.shape, ref.dtype)

    cst = _Consts()
    x = x_ref[...]
    h = _rms(x, mixg_ref[...]).astype(BF16)
    proj = _mm(h, win_ref[...])
    o = 0
    p_lx = proj[:, o:o + LRU_W]; o += LRU_W
    p_ly = proj[:, o:o + LRU_W]; o += LRU_W
    rw_in = 3 * MIX_W + 2 * LANES
    p_rw = proj[:, o:o + rw_in]; o += rw_in
    p_qkv = proj[:, o:o + 3 * MIX_W]; o += 3 * MIX_W
    p_z = proj[:, o:o + MIX_W]; o += MIX_W
    p_ab = proj[:, o:o + LANES]

    _lru(p_lx, p_ly, lru_prm, lru_carry, lru_h, t_idx, y_scr)
    v_first = _rwkv(p_rw, rwkv_prm, vres, rw_carry, rw_s, y_scr, LRU_W, cst)
    if vout_ref is not None:
        vout_ref[...] = v_first
    _gdn(p_qkv, p_z, p_ab, gdn_prm, gd_carry, gd_s, y_scr, LRU_W + MIX_W, cst)
    o_ref[...] = x + _mm(y_scr[...].astype(BF16), wout_ref[...])


def _mixer_call(x2, n_t, big, lru_prm, rwkv_prm, vres, gdn_prm):
    n, d = x2.shape
    mixg, win, wout = big
    has_vres = vres is not None
    row_map = lambda b, t: (b * n_t + t, 0)
    args = [x2, mixg, win, wout]
    specs = [pl.BlockSpec((TILE_T, d), row_map), _const_spec(mixg.shape), _const_spec(win.shape),
             _const_spec(wout.shape)]

    def add_consts(arrs):
        for a in arrs:
            args.append(a)
            specs.append(_const_spec(a.shape))

    add_consts(lru_prm)
    add_consts(rwkv_prm)
    if has_vres:
        args.append(vres[0])
        specs.append(pl.BlockSpec((TILE_T, MIX_W), row_map))
        add_consts(vres[1:])
    add_consts(gdn_prm)
    out_shape = [jax.ShapeDtypeStruct((n, d), F32)]
    out_specs = [pl.BlockSpec((TILE_T, d), row_map)]
    if not has_vres:
        out_shape.append(jax.ShapeDtypeStruct((n, MIX_W), F32))
        out_specs.append(pl.BlockSpec((TILE_T, MIX_W), row_map))
    rw_in = 3 * MIX_W + 2 * LANES
    scratch = [
        pltpu.VMEM((8, LRU_W), F32), pltpu.VMEM((8, LRU_W), F32),
        pltpu.VMEM((8, rw_in), F32), pltpu.VMEM((N_HEADS // 2, LANES, LANES), F32),
        pltpu.VMEM((8, 3 * MIX_W), F32), pltpu.VMEM((N_HEADS // 2, LANES, LANES), F32),
        pltpu.VMEM((TILE_T, d), F32),
    ]
    res = pl.pallas_call(
        functools.partial(_mixer_body, has_vres=has_vres),
        grid=(n // (n_t * TILE_T), n_t),
        in_specs=specs,
        out_specs=out_specs,
        out_shape=out_shape,
        scratch_shapes=scratch,
        compiler_params=pltpu.CompilerParams(dimension_semantics=("arbitrary", "arbitrary"),
                                             vmem_limit_bytes=VMEM_LIMIT),
        name="mixer_vres" if has_vres else "mixer",
    )(*args)
    return res[0], (vres[0] if has_vres else res[1])


def _row(v):
    return v.reshape(1, -1).astype(F32)


def _pad_lanes(v, width=LANES):
    v = v.reshape(1, -1).astype(F32)
    return jnp.pad(v, ((0, 0), (0, width - v.shape[1])))


def kernel(x, ffn1_norm, ffn1_wi, ffn1_wo, mix_norm, w_in, w_out, lru_conv_w, lru_conv_b, lru_gate_a_w, lru_gate_a_b, lru_gate_x_w, lru_gate_x_b, lru_lambda, lru_out_norm, rwkv_mu, rwkv_w_up, rwkv_w_bias, rwkv_a_up, rwkv_a_bias, rwkv_g_up, rwkv_k_k, rwkv_k_a, rwkv_r_k, rwkv_ln_g, rwkv_ln_b, rwkv_vres_w1, rwkv_vres_w2, rwkv_vres_b, gdn_conv_w, gdn_a_log, gdn_dt_bias, gdn_norm, ffn2_norm, ffn2_wi, ffn2_wo, final_norm):
    bsz, seq, d = x.shape
    depth = ffn1_norm.shape[0]
    assert seq % TILE_T == 0 and d == LRU_W + 2 * MIX_W
    n_t = seq // TILE_T
    x2 = x.reshape(bsz * seq, d)
    d_in = w_in.shape[-1]
    d_in_pad = -(-d_in // LANES) * LANES
    lora = rwkv_w_up.shape[1]
    assert 2 * lora == LANES and rwkv_a_up.shape[1] == lora and rwkv_g_up.shape[1] == LANES
    v_first = None
    for l in range(depth):
        x2 = _ffn_call(x2, ffn1_norm[l], ffn1_wi[l].astype(BF16), ffn1_wo[l].astype(BF16))
        big = (_row(mix_norm[l]), jnp.pad(w_in[l], ((0, 0), (0, d_in_pad - d_in))).astype(BF16), w_out[l].astype(BF16))
        gate_w = jnp.concatenate([jax.scipy.linalg.block_diag(*lru_gate_a_w[l]),
                                  jax.scipy.linalg.block_diag(*lru_gate_x_w[l])], axis=1).astype(BF16)
        gate_b = jnp.concatenate([lru_gate_a_b[l].reshape(1, -1), lru_gate_x_b[l].reshape(1, -1)], axis=1)
        lru_prm = [lru_conv_w[l], _row(lru_conv_b[l]), gate_w, gate_b, _row(lru_lambda[l]), _row(lru_out_norm[l])]
        wa_up = jnp.zeros((LANES, 2 * MIX_W), F32)
        wa_up = wa_up.at[:lora, :MIX_W].set(rwkv_w_up[l]).at[lora:, MIX_W:].set(rwkv_a_up[l]).astype(BF16)
        rwkv_prm = [_row(rwkv_mu[l]), wa_up, _row(rwkv_w_bias[l]), _row(rwkv_a_bias[l]), rwkv_g_up[l].astype(BF16),
                    _row(rwkv_k_k[l]), _row(rwkv_k_a[l]), _row(rwkv_r_k[l]), _row(rwkv_ln_g[l]), _row(rwkv_ln_b[l])]
        vres = None
        if l > 0:
            mv = rwkv_vres_w1.shape[-1]
            vw1 = jnp.pad(rwkv_vres_w1[l - 1], ((0, 0), (0, LANES - mv))).astype(BF16)
            vw2 = jnp.pad(rwkv_vres_w2[l - 1], ((0, LANES - mv), (0, 0))).astype(BF16)
            vres = [v_first, vw1, vw2, _row(rwkv_vres_b[l - 1])]
        gdn_prm = [gdn_conv_w[l], _pad_lanes(gdn_a_log[l]), _pad_lanes(gdn_dt_bias[l]),
                   _row(jnp.tile(gdn_norm[l], N_HEADS))]
        x2, v_first = _mixer_call(x2, n_t, big, lru_prm, rwkv_prm, vres, gdn_prm)
        last = l == depth - 1
        x2 = _ffn_call(x2, ffn2_norm[l], ffn2_wi[l].astype(BF16), ffn2_wo[l].astype(BF16),
                       gf=final_norm if last else None)
    return x2.reshape(bsz, seq, d)
```

```python
import functools

import jax
import jax.numpy as jnp
from jax import lax
from jax.experimental import pallas as pl
from jax.experimental.pallas import tpu as pltpu

F32 = jnp.float32
BF16 = jnp.bfloat16

NORM_EPS = 1e-6
L2_EPS = 1e-6
RWKV_GN_EPS = 64e-5
LRU_C = 8.0
HEAD_DIM = 64
N_HEADS = 6
MIX_W = N_HEADS * HEAD_DIM
LRU_W = 256
CHUNK = 64
LOG2_CHUNK = 6
TILE_T = 256
LANES = 128
FFN_TILE = 512
FFN_CHUNK = 256
VMEM_LIMIT = 56 * 1024 * 1024

P_TIME = 1
P_INV = 1
P_APPLY = 1
P_STATE = 1
P_EXACT = 3
P_SEG = 2

_NN = (((1,), (0,)), ((), ()))
_NT = (((1,), (1,)), ((), ()))
_TN = (((0,), (0,)), ((), ()))


def _pieces(x, n):
    if x.dtype == BF16:
        return [x]
    out = []
    r = x
    for i in range(n):
        p = r.astype(BF16)
        out.append(p)
        if i + 1 < n:
            r = r - p.astype(F32)
    return out


def _mm(a, b, dims=_NN, pa=1, pb=1):
    ap = _pieces(a, pa)
    bp = _pieces(b, pb)
    order = max(len(ap), len(bp))
    acc = None
    for i, x in enumerate(ap):
        for j, y in enumerate(bp):
            if i + j < order:
                t = lax.dot_general(x, y, dims, preferred_element_type=F32)
                acc = t if acc is None else acc + t
    return acc


def _lowp(x, pieces):
    return x.astype(BF16) if pieces == 1 else x


def _sigmoid(x):
    return 1.0 / (1.0 + jnp.exp(-x))


def _softplus(x):
    return jnp.maximum(x, 0.0) + jnp.log1p(jnp.exp(-jnp.abs(x)))


def _expm1(x):
    t = jnp.tanh(0.5 * x)
    return 2.0 * t / (1.0 - t)


def _gelu_tanh(x):
    return 0.5 * x * (1.0 + jnp.tanh(0.7978845608028654 * (x + 0.044715 * (x * x * x))))


def _rms(x, g, eps=NORM_EPS):
    return x * lax.rsqrt(jnp.mean(x * x, axis=-1, keepdims=True) + eps) * g


def _iota(shape, dim):
    return lax.broadcasted_iota(jnp.int32, shape, dim)


def _nil_inverse(lmat, eye, pieces):
    acc = jnp.where(eye, 1.0, lmat)
    power = lmat
    for _ in range(LOG2_CHUNK - 1):
        if pieces == 1:
            power = power.astype(BF16)
        power = _mm(power, power, pa=pieces, pb=pieces)
        acc = acc + _mm(acc, power, pa=pieces, pb=pieces)
    return acc


def _ffn_body(*refs, d_ff, final):
    if final:
        x_ref, g_ref, wi_ref, wo_ref, gf_ref, o_ref = refs
    else:
        x_ref, g_ref, wi_ref, wo_ref, o_ref = refs
    x = x_ref[...]
    h = _rms(x, g_ref[...]).astype(BF16)
    acc = jnp.zeros_like(x)
    for c in range(d_ff // FFN_CHUNK):
        lo = c * FFN_CHUNK
        gate = _mm(h, wi_ref[:, lo:lo + FFN_CHUNK])
        up = _mm(h, wi_ref[:, d_ff + lo:d_ff + lo + FFN_CHUNK])
        act = (gate * _sigmoid(gate) * up).astype(BF16)
        acc = acc + _mm(act, wo_ref[lo:lo + FFN_CHUNK, :])
    y = x + 0.5 * acc
    if final:
        y = _rms(y, gf_ref[...])
    o_ref[...] = y


def _const_spec(shape):
    return pl.BlockSpec(shape, lambda *_: (0,) * len(shape), pipeline_mode=pl.Buffered(1))


def _ffn_call(x2, g, wi, wo, gf=None):
    n, d = x2.shape
    d_ff = wo.shape[0]
    assert n % FFN_TILE == 0 and d_ff % FFN_CHUNK == 0
    final = gf is not None
    args = [x2, g.reshape(1, d), wi, wo]
    specs = [pl.BlockSpec((FFN_TILE, d), lambda i: (i, 0)), _const_spec((1, d)),
             _const_spec(wi.shape), _const_spec(wo.shape)]
    if final:
        args.append(gf.reshape(1, d))
        specs.append(_const_spec((1, d)))
    return pl.pallas_call(
        functools.partial(_ffn_body, d_ff=d_ff, final=final),
        grid=(n // FFN_TILE,),
        in_specs=specs,
        out_specs=pl.BlockSpec((FFN_TILE, d), lambda i: (i, 0)),
        out_shape=jax.ShapeDtypeStruct((n, d), F32),
        compiler_params=pltpu.CompilerParams(dimension_semantics=("arbitrary",), vmem_limit_bytes=VMEM_LIMIT),
        name="ffn_final" if final else "ffn",
    )(*args)


class _Consts:
    def __init__(self):
        tt = TILE_T
        row = _iota((tt, tt), 0)
        col = _iota((tt, tt), 1)
        same = (row >> LOG2_CHUNK) == (col >> LOG2_CHUNK)
        self.incl = same & (row >= col)
        self.strict = same & (row > col)
        self.eye = row == col
        self.cumsum = jnp.where(self.incl, 1.0, 0.0).astype(BF16)
        self.chunksum = jnp.where(same, 1.0, 0.0).astype(BF16)
        hr = _iota((MIX_W, MIX_W), 0)
        hc = _iota((MIX_W, MIX_W), 1)
        self.headsum = jnp.where((hr >> LOG2_CHUNK) == (hc >> LOG2_CHUNK), 1.0, 0.0).astype(BF16)
        pr = _iota((LANES, LANES), 0)
        pc = _iota((LANES, LANES), 1)
        self.pair_bd = (pr >> LOG2_CHUNK) == (pc >> LOG2_CHUNK)
        self.pair_eye = pr == pc
        self.lane = _iota((1, LANES), 1)
        self.half = (self.lane < HEAD_DIM, self.lane >= HEAD_DIM)
        er = _iota((LANES, MIX_W), 0)
        ec = _iota((LANES, MIX_W), 1) >> LOG2_CHUNK
        self.expand_g = jnp.where(er == ec, 1.0, 0.0).astype(BF16)
        self.expand_b = jnp.where(er == ec + N_HEADS, 1.0, 0.0).astype(BF16)


def _head_sum(x, cst):
    return _mm(x, cst.headsum, pa=P_SEG)


def _lru(px, py, prm, carry_ref, h_ref, t_idx, y_ref):
    conv_w, conv_b, gate_w, gate_b, lam, out_g = prm
    tt = px.shape[0]
    xx = jnp.concatenate([carry_ref[...], px], axis=0)
    carry_ref[...] = px[tt - 8:, :]
    xc = conv_w[3:4, :] * px + conv_b
    for j in (1, 2, 3):
        xc = xc + conv_w[3 - j:4 - j, :] * pltpu.roll(xx, j, axis=0)[8:, :]
    gates = _mm(xc.astype(BF16), gate_w) + gate_b
    r = _sigmoid(gates[:, :LRU_W])
    i = _sigmoid(gates[:, LRU_W:])
    log_a = -LRU_C * r * _softplus(-lam)
    a = jnp.exp(log_a)
    mult = jnp.sqrt(-_expm1(2.0 * log_a))
    rid = _iota((tt, LRU_W), 0)
    mult = jnp.where((rid == 0) & (t_idx == 0), 1.0, mult)
    b = mult * i * xc
    d = 1
    while d < tt:
        keep = rid >= d
        a_prev = jnp.where(keep, pltpu.roll(a, d, axis=0), 1.0)
        b_prev = jnp.where(keep, pltpu.roll(b, d, axis=0), 0.0)
        b = a * b_prev + b
        a = a * a_prev
        d *= 2
    h = b + a * h_ref[7:8, :]
    h_ref[...] = h[tt - 8:, :]
    y_ref[:, 0:LRU_W] = _rms(h * _gelu_tanh(py), out_g)


def _rwkv(p, prm, vres, carry_ref, s_ref, y_ref, y_off, cst):
    mu, wa_up, w_bias, a_bias, g_up, k_k, k_a, r_k, ln_g, ln_b = prm
    tt = p.shape[0]
    w = MIX_W
    rid = _iota((tt, 1), 0)
    xs = jnp.where(rid == 0, carry_ref[7:8, :], pltpu.roll(p, 1, axis=0))
    carry_ref[...] = p[tt - 8:, :]
    xm = p + (xs - p) * mu
    r = xm[:, 0:w]
    k = xm[:, w:2 * w]
    v = xm[:, 2 * w:3 * w]
    x_lora = xm[:, 3 * w:3 * w + LANES]
    xg = xm[:, 3 * w + LANES:3 * w + 2 * LANES]
    z = jnp.where(cst.half[0], jnp.tanh(x_lora), x_lora)
    lwa = _mm(z.astype(BF16), wa_up)
    w_log = -_softplus(-(w_bias + lwa[:, :w])) - 0.5
    logw = -jnp.exp(w_log)
    a = _sigmoid(a_bias + lwa[:, w:])
    g = _mm(_sigmoid(xg).astype(BF16), g_up)
    v_first = v
    if vres is not None:
        vf, vw1, vw2, vb = vres
        lo = _mm(_mm(v.astype(BF16), vw1).astype(BF16), vw2)
        v = v + (vf - v) * _sigmoid(vb + lo)
    kk = k * k_k
    kk = kk * lax.rsqrt(_head_sum(kk * kk, cst) + L2_EPS)
    k2 = k * (1.0 + (a - 1.0) * k_a)
    bb = kk * a

    cum = _mm(cst.cumsum, logw, pb=P_EXACT)
    tot = _mm(cst.chunksum, logw, pb=P_EXACT)
    inv_w = jnp.exp(-cum)
    rem = jnp.exp(tot - cum)
    wc = jnp.exp(tot)
    r_t = r * jnp.exp(cum)
    a_t = -kk * jnp.exp(cum - logw)
    b_t = bb * inv_w
    k_t = k2 * inv_w
    b_w = bb * rem
    k_w = k2 * rem

    for j in range(N_HEADS // 2):
        sl = slice(j * LANES, (j + 1) * LANES)
        a_p, r_p, b_p, k_p, v_p, bw_p, kw_p = (t[:, sl] for t in (a_t, r_t, b_t, k_t, v, b_w, k_w))
        x2 = _lowp(jnp.concatenate([b_p, k_p], axis=0), P_TIME)
        v_lp = _lowp(v_p, P_APPLY)
        per_head = []
        for half in range(2):
            m = cst.half[half]
            x1 = jnp.concatenate([jnp.where(m, a_p, 0.0), jnp.where(m, r_p, 0.0)], axis=0)
            gram = _mm(x1, x2, _NT, pa=P_TIME, pb=P_TIME)
            g_ab = jnp.where(cst.strict, gram[:tt, :tt], 0.0)
            g_ak = jnp.where(cst.strict, gram[:tt, tt:], 0.0)
            m_rb = jnp.where(cst.incl, gram[tt:, :tt], 0.0)
            m_rk = jnp.where(cst.incl, gram[tt:, tt:], 0.0)
            t_inv = _nil_inverse(g_ab, cst.eye, P_INV)
            gv = _mm(jnp.concatenate([g_ak, m_rk], axis=0), v_lp, pa=P_APPLY, pb=P_APPLY)
            ut = _mm(t_inv, jnp.concatenate([gv[:tt], a_p], axis=1), pa=P_APPLY, pb=P_APPLY)
            mm = _mm(m_rb, ut, pa=P_APPLY, pb=P_APPLY)
            per_head.append((ut[:, :LANES], ut[:, LANES:], mm[:, :LANES] + gv[tt:], r_p + mm[:, LANES:]))
        u0, ta, y0, ra = (jnp.where(cst.half[0], x0, x1) for x0, x1 in zip(*per_head))
        s = s_ref[j]
        for c in range(tt // CHUNK):
            rc = slice(c * CHUNK, (c + 1) * CHUNK)
            phi = jnp.where(cst.pair_eye, wc[c * CHUNK:c * CHUNK + 1, sl], 0.0) + jnp.where(
                cst.pair_bd, _mm(ta[rc], bw_p[rc], _TN, pa=P_STATE, pb=P_STATE), 0.0)
            delta = jnp.where(cst.pair_bd, _mm(jnp.concatenate([u0[rc], v_p[rc]], axis=0),
                                               jnp.concatenate([bw_p[rc], kw_p[rc]], axis=0),
                                               _TN, pa=P_STATE, pb=P_STATE), 0.0)
            y_ref[rc, y_off + j * LANES:y_off + (j + 1) * LANES] = (
                _mm(ra[rc], s, _NT, pa=P_STATE, pb=P_STATE) + y0[rc])
            s = _mm(s, phi, pa=P_STATE, pb=P_STATE) + delta
        s_ref[j] = s

    y = y_ref[:, y_off:y_off + w]
    mean = _head_sum(y, cst) * (1.0 / HEAD_DIM)
    yc = y - mean
    var = _head_sum(yc * yc, cst) * (1.0 / HEAD_DIM)
    yn = yc * lax.rsqrt(var + RWKV_GN_EPS) * ln_g + ln_b
    bonus = _head_sum(r * k2 * r_k, cst) * v
    y_ref[:, y_off:y_off + w] = (yn + bonus) * g
    return v_first


def _gdn(p_qkv, p_z, p_ab, prm, carry_ref, s_ref, y_ref, y_off, cst):
    conv_w, a_log, dt_bias, norm_g = prm
    tt = p_qkv.shape[0]
    w = MIX_W
    xx = jnp.concatenate([carry_ref[...], p_qkv], axis=0)
    carry_ref[...] = p_qkv[tt - 8:, :]
    conv = conv_w[3:4, :] * p_qkv
    for j in (1, 2, 3):
        conv = conv + conv_w[3 - j:4 - j, :] * pltpu.roll(xx, j, axis=0)[8:, :]
    qkv = conv * _sigmoid(conv)
    q = qkv[:, 0:w]
    k = qkv[:, w:2 * w]
    v = qkv[:, 2 * w:3 * w]
    q = q * lax.rsqrt(_head_sum(q * q, cst) + L2_EPS) * (HEAD_DIM ** -0.5)
    k = k * lax.rsqrt(_head_sum(k * k, cst) + L2_EPS)
    gb = jnp.where(cst.lane < N_HEADS, -jnp.exp(a_log) * _softplus(p_ab + dt_bias), _sigmoid(p_ab))
    beta = _mm(gb, cst.expand_b, pa=P_EXACT)
    gcs = _mm(cst.cumsum, gb, pb=P_EXACT)
    tots = _mm(cst.chunksum, gb, pb=P_EXACT)
    gcs_t = gcs.T
    gc = _mm(gcs, cst.expand_g, pa=P_EXACT)
    tot = _mm(tots, cst.expand_g, pa=P_EXACT)
    egc = jnp.exp(gc)
    kb = k * beta
    vb = v * beta
    kbe = kb * egc
    qd = q * egc
    kd = k * jnp.exp(tot - gc)
    gl = jnp.exp(tot)

    for j in range(N_HEADS // 2):
        sl = slice(j * LANES, (j + 1) * LANES)
        kb_p, q_p, k_p, vb_p, kbe_p, qd_p, kd_p = (t[:, sl] for t in (kb, q, k, vb, kbe, qd, kd))
        rhs = _lowp(jnp.concatenate([vb_p, kbe_p], axis=1), P_APPLY)
        k_lp = _lowp(k_p, P_TIME)
        per_head = []
        for half in range(2):
            h = 2 * j + half
            m = cst.half[half]
            diff = gcs[:, h:h + 1] - gcs_t[h:h + 1, :]
            dec = jnp.where(cst.incl, jnp.exp(jnp.where(cst.incl, diff, 0.0)), 0.0)
            x1 = jnp.concatenate([jnp.where(m, kb_p, 0.0), jnp.where(m, q_p, 0.0)], axis=0)
            gram = _mm(x1, k_lp, _NT, pa=P_TIME, pb=P_TIME)
            a_mat = jnp.where(cst.strict, gram[:tt] * dec, 0.0)
            qk = gram[tt:] * dec
            t_inv = _nil_inverse(-a_mat, cst.eye, P_INV)
            uw = _mm(t_inv, rhs, pa=P_APPLY, pb=P_APPLY)
            oq = _mm(qk, uw, pa=P_APPLY, pb=P_APPLY)
            per_head.append((uw[:, :LANES], uw[:, LANES:], oq[:, :LANES], qd_p - oq[:, LANES:]))
        u, wm, o0, qa = (jnp.where(cst.half[0], x0, x1) for x0, x1 in zip(*per_head))
        s = s_ref[j]
        for c in range(tt // CHUNK):
            rc = slice(c * CHUNK, (c + 1) * CHUNK)
            ku = _mm(kd_p[rc], jnp.concatenate([u[rc], wm[rc]], axis=1), _TN, pa=P_STATE, pb=P_STATE)
            delta = jnp.where(cst.pair_bd, ku[:, :LANES], 0.0)
            phi = jnp.where(cst.pair_eye, gl[c * CHUNK:c * CHUNK + 1, sl], 0.0) - jnp.where(
                cst.pair_bd, ku[:, LANES:], 0.0)
            y_ref[rc, y_off + j * LANES:y_off + (j + 1) * LANES] = (
                _mm(qa[rc], s, pa=P_STATE, pb=P_STATE) + o0[rc])
            s = _mm(phi, s, pa=P_STATE, pb=P_STATE) + delta
        s_ref[j] = s

    o = y_ref[:, y_off:y_off + w]
    on = o * lax.rsqrt(_head_sum(o * o, cst) * (1.0 / HEAD_DIM) + NORM_EPS) * norm_g
    y_ref[:, y_off:y_off + w] = on * (p_z * _sigmoid(p_z))


_N_LRU, _N_RWKV, _N_VRES, _N_GDN = 6, 10, 3, 4


def _mixer_body(*refs, has_vres):
    it = iter(refs)
    take = lambda n: [next(it) for _ in range(n)]
    x_ref, mixg_ref, win_ref, wout_ref = take(4)
    lru_prm = [r[...] for r in take(_N_LRU)]
    rwkv_prm = [r[...] for r in take(_N_RWKV)]
    vres = None
    if has_vres:
        vf_ref, = take(1)
        vres = [vf_ref[...]] + [r[...] for r in take(_N_VRES)]
    gdn_prm = [r[...] for r in take(_N_GDN)]
    o_ref, = take(1)
    vout_ref = None
    if not has_vres:
        vout_ref, = take(1)
    lru_carry, lru_h, rw_carry, rw_s, gd_carry, gd_s, y_scr = take(7)

    t_idx = pl.program_id(1)

    @pl.when(t_idx == 0)
    def _():
        for ref in (lru_carry, lru_h, rw_carry, rw_s, gd_carry, gd_s):
            ref[...] = jnp.zeros(ref.shape, ref.dtype)

    cst = _Consts()
    x = x_ref[...]
    h = _rms(x, mixg_ref[...]).astype(BF16)
    proj = _mm(h, win_ref[...])
    o = 0
    p_lx = proj[:, o:o + LRU_W]; o += LRU_W
    p_ly = proj[:, o:o + LRU_W]; o += LRU_W
    rw_in = 3 * MIX_W + 2 * LANES
    p_rw = proj[:, o:o + rw_in]; o += rw_in
    p_qkv = proj[:, o:o + 3 * MIX_W]; o += 3 * MIX_W
    p_z = proj[:, o:o + MIX_W]; o += MIX_W
    p_ab = proj[:, o:o + LANES]

    _lru(p_lx, p_ly, lru_prm, lru_carry, lru_h, t_idx, y_scr)
    v_first = _rwkv(p_rw, rwkv_prm, vres, rw_carry, rw_s, y_scr, LRU_W, cst)
    if vout_ref is not None:
        vout_ref[...] = v_first
    _gdn(p_qkv, p_z, p_ab, gdn_prm, gd_carry, gd_s, y_scr, LRU_W + MIX_W, cst)
    o_ref[...] = x + _mm(y_scr[...].astype(BF16), wout_ref[...])


def _mixer_call(x2, n_t, big, lru_prm, rwkv_prm, vres, gdn_prm):
    n, d = x2.shape
    mixg, win, wout = big
    has_vres = vres is not None
    row_map = lambda b, t: (b * n_t + t, 0)
    args = [x2, mixg, win, wout]
    specs = [pl.BlockSpec((TILE_T, d), row_map), _const_spec(mixg.shape), _const_spec(win.shape),
             _const_spec(wout.shape)]

    def add_consts(arrs):
        for a in arrs:
            args.append(a)
            specs.append(_const_spec(a.shape))

    add_consts(lru_prm)
    add_consts(rwkv_prm)
    if has_vres:
        args.append(vres[0])
        specs.append(pl.BlockSpec((TILE_T, MIX_W), row_map))
        add_consts(vres[1:])
    add_consts(gdn_prm)
    out_shape = [jax.ShapeDtypeStruct((n, d), F32)]
    out_specs = [pl.BlockSpec((TILE_T, d), row_map)]
    if not has_vres:
        out_shape.append(jax.ShapeDtypeStruct((n, MIX_W), F32))
        out_specs.append(pl.BlockSpec((TILE_T, MIX_W), row_map))
    rw_in = 3 * MIX_W + 2 * LANES
    scratch = [
        pltpu.VMEM((8, LRU_W), F32), pltpu.VMEM((8, LRU_W), F32),
        pltpu.VMEM((8, rw_in), F32), pltpu.VMEM((N_HEADS // 2, LANES, LANES), F32),
        pltpu.VMEM((8, 3 * MIX_W), F32), pltpu.VMEM((N_HEADS // 2, LANES, LANES), F32),
        pltpu.VMEM((TILE_T, d), F32),
    ]
    res = pl.pallas_call(
        functools.partial(_mixer_body, has_vres=has_vres),
        grid=(n // (n_t * TILE_T), n_t),
        in_specs=specs,
        out_specs=out_specs,
        out_shape=out_shape,
        scratch_shapes=scratch,
        compiler_params=pltpu.CompilerParams(dimension_semantics=("arbitrary", "arbitrary"),
                                             vmem_limit_bytes=VMEM_LIMIT),
        name="mixer_vres" if has_vres else "mixer",
    )(*args)
    return res[0], (vres[0] if has_vres else res[1])


def _row(v):
    return v.reshape(1, -1).astype(F32)


def _pad_lanes(v, width=LANES):
    v = v.reshape(1, -1).astype(F32)
    return jnp.pad(v, ((0, 0), (0, width - v.shape[1])))


def kernel(x, ffn1_norm, ffn1_wi, ffn1_wo, mix_norm, w_in, w_out, lru_conv_w, lru_conv_b, lru_gate_a_w, lru_gate_a_b, lru_gate_x_w, lru_gate_x_b, lru_lambda, lru_out_norm, rwkv_mu, rwkv_w_up, rwkv_w_bias, rwkv_a_up, rwkv_a_bias, rwkv_g_up, rwkv_k_k, rwkv_k_a, rwkv_r_k, rwkv_ln_g, rwkv_ln_b, rwkv_vres_w1, rwkv_vres_w2, rwkv_vres_b, gdn_conv_w, gdn_a_log, gdn_dt_bias, gdn_norm, ffn2_norm, ffn2_wi, ffn2_wo, final_norm):
    bsz, seq, d = x.shape
    depth = ffn1_norm.shape[0]
    assert seq % TILE_T == 0 and d == LRU_W + 2 * MIX_W
    n_t = seq // TILE_T
    x2 = x.reshape(bsz * seq, d)
    d_in = w_in.shape[-1]
    d_in_pad = -(-d_in // LANES) * LANES
    lora = rwkv_w_up.shape[1]
    assert 2 * lora == LANES and rwkv_a_up.shape[1] == lora and rwkv_g_up.shape[1] == LANES
    v_first = None
    for l in range(depth):
        x2 = _ffn_call(x2, ffn1_norm[l], ffn1_wi[l].astype(BF16), ffn1_wo[l].astype(BF16))
        big = (_row(mix_norm[l]), jnp.pad(w_in[l], ((0, 0), (0, d_in_pad - d_in))).astype(BF16), w_out[l].astype(BF16))
        gate_w = jnp.concatenate([jax.scipy.linalg.block_diag(*lru_gate_a_w[l]),
                                  jax.scipy.linalg.block_diag(*lru_gate_x_w[l])], axis=1).astype(BF16)
        gate_b = jnp.concatenate([lru_gate_a_b[l].reshape(1, -1), lru_gate_x_b[l].reshape(1, -1)], axis=1)
        lru_prm = [lru_conv_w[l], _row(lru_conv_b[l]), gate_w, gate_b, _row(lru_lambda[l]), _row(lru_out_norm[l])]
        wa_up = jnp.zeros((LANES, 2 * MIX_W), F32)
        wa_up = wa_up.at[:lora, :MIX_W].set(rwkv_w_up[l]).at[lora:, MIX_W:].set(rwkv_a_up[l]).astype(BF16)
        rwkv_prm = [_row(rwkv_mu[l]), wa_up, _row(rwkv_w_bias[l]), _row(rwkv_a_bias[l]), rwkv_g_up[l].astype(BF16),
                    _row(rwkv_k_k[l]), _row(rwkv_k_a[l]), _row(rwkv_r_k[l]), _row(rwkv_ln_g[l]), _row(rwkv_ln_b[l])]
        vres = None
        if l > 0:
            mv = rwkv_vres_w1.shape[-1]
            vw1 = jnp.pad(rwkv_vres_w1[l - 1], ((0, 0), (0, LANES - mv))).astype(BF16)
            vw2 = jnp.pad(rwkv_vres_w2[l - 1], ((0, LANES - mv), (0, 0))).astype(BF16)
            vres = [v_first, vw1, vw2, _row(rwkv_vres_b[l - 1])]
        gdn_prm = [gdn_conv_w[l], _pad_lanes(gdn_a_log[l]), _pad_lanes(gdn_dt_bias[l]),
                   _row(jnp.tile(gdn_norm[l], N_HEADS))]
        x2, v_first = _mixer_call(x2, n_t, big, lru_prm, rwkv_prm, vres, gdn_prm)
        last = l == depth - 1
        x2 = _ffn_call(x2, ffn2_norm[l], ffn2_wi[l].astype(BF16), ffn2_wo[l].astype(BF16),
                       gf=final_norm if last else None)
    return x2.reshape(bsz, seq, d)
```

```python
import functools

import jax
import jax.numpy as jnp
from jax import lax
from jax.experimental import pallas as pl
from jax.experimental.pallas import tpu as pltpu

F32 = jnp.float32
BF16 = jnp.bfloat16

NORM_EPS = 1e-6
L2_EPS = 1e-6
RWKV_GN_EPS = 64e-5
LRU_C = 8.0
HEAD_DIM = 64
N_HEADS = 6
MIX_W = N_HEADS * HEAD_DIM
LRU_W = 256
CHUNK = 64
LOG2_CHUNK = 6
TILE_T = 256
LANES = 128
FFN_TILE = 512
FFN_CHUNK = 256
VMEM_LIMIT = 56 * 1024 * 1024

P_TIME = 1
P_INV = 1
P_APPLY = 1
P_STATE = 1
P_EXACT = 3
P_SEG = 2

_NN = (((1,), (0,)), ((), ()))
_NT = (((1,), (1,)), ((), ()))
_TN = (((0,), (0,)), ((), ()))


def _pieces(x, n):
    if x.dtype == BF16:
        return [x]
    out = []
    r = x
    for i in range(n):
        p = r.astype(BF16)
        out.append(p)
        if i + 1 < n:
            r = r - p.astype(F32)
    return out


def _mm(a, b, dims=_NN, pa=1, pb=1):
    ap = _pieces(a, pa)
    bp = _pieces(b, pb)
    order = max(len(ap), len(bp))
    acc = None
    for i, x in enumerate(ap):
        for j, y in enumerate(bp):
            if i + j < order:
                t = lax.dot_general(x, y, dims, preferred_element_type=F32)
                acc = t if acc is None else acc + t
    return acc


def _lowp(x, pieces):
    return x.astype(BF16) if pieces == 1 else x


def _sigmoid(x):
    return 1.0 / (1.0 + jnp.exp(-x))


def _softplus(x):
    return jnp.maximum(x, 0.0) + jnp.log1p(jnp.exp(-jnp.abs(x)))


def _expm1(x):
    t = jnp.tanh(0.5 * x)
    return 2.0 * t / (1.0 - t)


def _gelu_tanh(x):
    return 0.5 * x * (1.0 + jnp.tanh(0.7978845608028654 * (x + 0.044715 * (x * x * x))))


def _rms(x, g, eps=NORM_EPS):
    return x * lax.rsqrt(jnp.mean(x * x, axis=-1, keepdims=True) + eps) * g


def _iota(shape, dim):
    return lax.broadcasted_iota(jnp.int32, shape, dim)


def _ffn_body(*refs, d_ff, final):
    if final:
        x_ref, g_ref, wi_ref, wo_ref, gf_ref, o_ref = refs
    else:
        x_ref, g_ref, wi_ref, wo_ref, o_ref = refs
    x = x_ref[...]
    h = _rms(x, g_ref[...]).astype(BF16)
    acc = jnp.zeros_like(x)
    for c in range(d_ff // FFN_CHUNK):
        lo = c * FFN_CHUNK
        gate = _mm(h, wi_ref[:, lo:lo + FFN_CHUNK])
        up = _mm(h, wi_ref[:, d_ff + lo:d_ff + lo + FFN_CHUNK])
        act = (gate * _sigmoid(gate) * up).astype(BF16)
        acc = acc + _mm(act, wo_ref[lo:lo + FFN_CHUNK, :])
    y = x + 0.5 * acc
    if final:
        y = _rms(y, gf_ref[...])
    o_ref[...] = y


def _const_spec(shape):
    return pl.BlockSpec(shape, lambda *_: (0,) * len(shape), pipeline_mode=pl.Buffered(1))


def _ffn_call(x2, g, wi, wo, gf=None):
    n, d = x2.shape
    d_ff = wo.shape[0]
    assert n % FFN_TILE == 0 and d_ff % FFN_CHUNK == 0
    final = gf is not None
    args = [x2, g.reshape(1, d), wi, wo]
    specs = [pl.BlockSpec((FFN_TILE, d), lambda i: (i, 0)), _const_spec((1, d)),
             _const_spec(wi.shape), _const_spec(wo.shape)]
    if final:
        args.append(gf.reshape(1, d))
        specs.append(_const_spec((1, d)))
    return pl.pallas_call(
        functools.partial(_ffn_body, d_ff=d_ff, final=final),
        grid=(n // FFN_TILE,),
        in_specs=specs,
        out_specs=pl.BlockSpec((FFN_TILE, d), lambda i: (i, 0)),
        out_shape=jax.ShapeDtypeStruct((n, d), F32),
        compiler_params=pltpu.CompilerParams(dimension_semantics=("arbitrary",), vmem_limit_bytes=VMEM_LIMIT),
        name="ffn_final" if final else "ffn",
    )(*args)


class _Consts:
    def __init__(self):
        tt = TILE_T
        row = _iota((tt, tt), 0)
        col = _iota((tt, tt), 1)
        same = (row >> LOG2_CHUNK) == (col >> LOG2_CHUNK)
        self.incl = same & (row >= col)
        self.strict = same & (row > col)
        self.eye = row == col
        self.cumsum = jnp.where(self.incl, 1.0, 0.0).astype(BF16)
        self.chunksum = jnp.where(same, 1.0, 0.0).astype(BF16)
        hr = _iota((MIX_W, MIX_W), 0)
        hc = _iota((MIX_W, MIX_W), 1)
        self.headsum = jnp.where((hr >> LOG2_CHUNK) == (hc >> LOG2_CHUNK), 1.0, 0.0).astype(BF16)
        pr = _iota((LANES, LANES), 0)
        pc = _iota((LANES, LANES), 1)
        self.pair_bd = (pr >> LOG2_CHUNK) == (pc >> LOG2_CHUNK)
        self.pair_eye = pr == pc
        self.lane = _iota((1, LANES), 1)
        self.half = (self.lane < HEAD_DIM, self.lane >= HEAD_DIM)
        er = _iota((LANES, MIX_W), 0)
        ec = _iota((LANES, MIX_W), 1) >> LOG2_CHUNK
        self.expand_g = jnp.where(er == ec, 1.0, 0.0).astype(BF16)
        self.expand_b = jnp.where(er == ec + N_HEADS, 1.0, 0.0).astype(BF16)


def _head_sum(x, cst):
    return _mm(x, cst.headsum, pa=P_SEG)


def _lru(px, py, prm, carry_ref, h_ref, t_idx, y_ref):
    conv_w, conv_b, gate_w, gate_b, lam, out_g = prm
    tt = px.shape[0]
    xx = jnp.concatenate([carry_ref[...], px], axis=0)
    carry_ref[...] = px[tt - 8:, :]
    xc = conv_w[3:4, :] * px + conv_b
    for j in (1, 2, 3):
        xc = xc + conv_w[3 - j:4 - j, :] * pltpu.roll(xx, j, axis=0)[8:, :]
    gates = _mm(xc.astype(BF16), gate_w) + gate_b
    r = _sigmoid(gates[:, :LRU_W])
    i = _sigmoid(gates[:, LRU_W:])
    log_a = -LRU_C * r * _softplus(-lam)
    a = jnp.exp(log_a)
    mult = jnp.sqrt(-_expm1(2.0 * log_a))
    rid = _iota((tt, LRU_W), 0)
    mult = jnp.where((rid == 0) & (t_idx == 0), 1.0, mult)
    b = mult * i * xc
    d = 1
    while d < tt:
        keep = rid >= d
        a_prev = jnp.where(keep, pltpu.roll(a, d, axis=0), 1.0)
        b_prev = jnp.where(keep, pltpu.roll(b, d, axis=0), 0.0)
        b = a * b_prev + b
        a = a * a_prev
        d *= 2
    h = b + a * h_ref[7:8, :]
    h_ref[...] = h[tt - 8:, :]
    y_ref[:, 0:LRU_W] = _rms(h * _gelu_tanh(py), out_g)


def _nil_inverse_many(lmats, eye, pieces):
    accs = [jnp.where(eye, 1.0, l) for l in lmats]
    powers = list(lmats)
    for _ in range(LOG2_CHUNK - 1):
        if pieces == 1:
            powers = [p.astype(BF16) for p in powers]
        powers = [_mm(p, p, pa=pieces, pb=pieces) for p in powers]
        accs = [a + _mm(a, p, pa=pieces, pb=pieces) for a, p in zip(accs, powers)]
    return accs


def _pair_slice(j):
    return slice(j * LANES, (j + 1) * LANES)


def _merge_halves(per_head, cst):
    return [tuple(jnp.where(cst.half[0], x0, x1) for x0, x1 in zip(per_head[2 * j], per_head[2 * j + 1]))
            for j in range(N_HEADS // 2)]


class _Rwkv:
    def __init__(self, p, prm, vres, carry_ref, cst):
        mu, wa_up, w_bias, a_bias, g_up, k_k, k_a, r_k, self.ln_g, self.ln_b = prm
        self.cst = cst
        tt = p.shape[0]
        w = MIX_W
        rid = _iota((tt, 1), 0)
        xs = jnp.where(rid == 0, carry_ref[7:8, :], pltpu.roll(p, 1, axis=0))
        carry_ref[...] = p[tt - 8:, :]
        xm = p + (xs - p) * mu
        r = xm[:, 0:w]
        k = xm[:, w:2 * w]
        v = xm[:, 2 * w:3 * w]
        x_lora = xm[:, 3 * w:3 * w + LANES]
        xg = xm[:, 3 * w + LANES:3 * w + 2 * LANES]
        z = jnp.where(cst.half[0], jnp.tanh(x_lora), x_lora)
        lwa = _mm(z.astype(BF16), wa_up)
        w_log = -_softplus(-(w_bias + lwa[:, :w])) - 0.5
        logw = -jnp.exp(w_log)
        a = _sigmoid(a_bias + lwa[:, w:])
        self.g = _mm(_sigmoid(xg).astype(BF16), g_up)
        self.v_first = v
        if vres is not None:
            vf, vw1, vw2, vb = vres
            lo = _mm(_mm(v.astype(BF16), vw1).astype(BF16), vw2)
            v = v + (vf - v) * _sigmoid(vb + lo)
        kk = k * k_k
        kk = kk * lax.rsqrt(_head_sum(kk * kk, cst) + L2_EPS)
        k2 = k * (1.0 + (a - 1.0) * k_a)
        bb = kk * a
        cum = _mm(cst.cumsum, logw, pb=P_EXACT)
        tot = _mm(cst.chunksum, logw, pb=P_EXACT)
        inv_w = jnp.exp(-cum)
        rem = jnp.exp(tot - cum)
        self.wc = jnp.exp(tot)
        self.r_t = r * jnp.exp(cum)
        self.a_t = -kk * jnp.exp(cum - logw)
        self.b_t = bb * inv_w
        self.k_t = k2 * inv_w
        self.b_w = bb * rem
        self.k_w = k2 * rem
        self.v = v
        self.bonus_rk = r * k2 * r_k
        self.tt = tt

    def gram(self):
        cst, tt = self.cst, self.tt
        self.m_rb, self.gv = [], []
        lmats = []
        for j in range(N_HEADS // 2):
            sl = _pair_slice(j)
            a_p, r_p, b_p, k_p = (t[:, sl] for t in (self.a_t, self.r_t, self.b_t, self.k_t))
            x2 = _lowp(jnp.concatenate([b_p, k_p], axis=0), P_TIME)
            v_lp = _lowp(self.v[:, sl], P_APPLY)
            for half in range(2):
                m = cst.half[half]
                x1 = jnp.concatenate([jnp.where(m, a_p, 0.0), jnp.where(m, r_p, 0.0)], axis=0)
                gram = _mm(x1, x2, _NT, pa=P_TIME, pb=P_TIME)
                lmats.append(jnp.where(cst.strict, gram[:tt, :tt], 0.0))
                g_ak = jnp.where(cst.strict, gram[:tt, tt:], 0.0)
                self.m_rb.append(_lowp(jnp.where(cst.incl, gram[tt:, :tt], 0.0), P_APPLY))
                m_rk = jnp.where(cst.incl, gram[tt:, tt:], 0.0)
                self.gv.append(_mm(jnp.concatenate([g_ak, m_rk], axis=0), v_lp, pa=P_APPLY, pb=P_APPLY))
        return lmats

    def apply(self, t_invs):
        cst, tt = self.cst, self.tt
        uts = []
        for h, t_inv in enumerate(t_invs):
            a_p = self.a_t[:, _pair_slice(h // 2)]
            uts.append(_mm(t_inv, jnp.concatenate([self.gv[h][:tt], a_p], axis=1), pa=P_APPLY, pb=P_APPLY))
        per_head = []
        for h, ut in enumerate(uts):
            r_p = self.r_t[:, _pair_slice(h // 2)]
            mm = _mm(self.m_rb[h], ut, pa=P_APPLY, pb=P_APPLY)
            per_head.append((ut[:, :LANES], ut[:, LANES:], mm[:, :LANES] + self.gv[h][tt:], r_p + mm[:, LANES:]))
        self.pairs = _merge_halves(per_head, cst)

    def transitions(self, j, c):
        cst = self.cst
        sl = _pair_slice(j)
        rc = slice(c * CHUNK, (c + 1) * CHUNK)
        u0, ta, _, _ = self.pairs[j]
        phi = jnp.where(cst.pair_eye, self.wc[c * CHUNK:c * CHUNK + 1, sl], 0.0) + jnp.where(
            cst.pair_bd, _mm(ta[rc], self.b_w[rc, sl], _TN, pa=P_STATE, pb=P_STATE), 0.0)
        delta = jnp.where(cst.pair_bd, _mm(jnp.concatenate([u0[rc], self.v[rc, sl]], axis=0),
                                           jnp.concatenate([self.b_w[rc, sl], self.k_w[rc, sl]], axis=0),
                                           _TN, pa=P_STATE, pb=P_STATE), 0.0)
        return phi, delta

    def step(self, j, c, s, trans, y_ref, y_off):
        rc = slice(c * CHUNK, (c + 1) * CHUNK)
        _, _, y0, ra = self.pairs[j]
        phi, delta = trans
        y_ref[rc, y_off + j * LANES:y_off + (j + 1) * LANES] = (
            _mm(ra[rc], s, _NT, pa=P_STATE, pb=P_STATE) + y0[rc])
        return _mm(s, phi, pa=P_STATE, pb=P_STATE) + delta

    def finish(self, y_ref, y_off):
        cst, w = self.cst, MIX_W
        y = y_ref[:, y_off:y_off + w]
        mean = _head_sum(y, cst) * (1.0 / HEAD_DIM)
        yc = y - mean
        var = _head_sum(yc * yc, cst) * (1.0 / HEAD_DIM)
        yn = yc * lax.rsqrt(var + RWKV_GN_EPS) * self.ln_g + self.ln_b
        bonus = _head_sum(self.bonus_rk, cst) * self.v
        y_ref[:, y_off:y_off + w] = (yn + bonus) * self.g


class _Gdn:
    def __init__(self, p_qkv, p_z, p_ab, prm, carry_ref, cst):
        conv_w, a_log, dt_bias, self.norm_g = prm
        self.cst = cst
        self.p_z = p_z
        tt = p_qkv.shape[0]
        w = MIX_W
        xx = jnp.concatenate([carry_ref[...], p_qkv], axis=0)
        carry_ref[...] = p_qkv[tt - 8:, :]
        conv = conv_w[3:4, :] * p_qkv
        for j in (1, 2, 3):
            conv = conv + conv_w[3 - j:4 - j, :] * pltpu.roll(xx, j, axis=0)[8:, :]
        qkv = conv * _sigmoid(conv)
        q = qkv[:, 0:w]
        k = qkv[:, w:2 * w]
        v = qkv[:, 2 * w:3 * w]
        q = q * lax.rsqrt(_head_sum(q * q, cst) + L2_EPS) * (HEAD_DIM ** -0.5)
        k = k * lax.rsqrt(_head_sum(k * k, cst) + L2_EPS)
        gb = jnp.where(cst.lane < N_HEADS, -jnp.exp(a_log) * _softplus(p_ab + dt_bias), _sigmoid(p_ab))
        beta = _mm(gb, cst.expand_b, pa=P_EXACT)
        self.gcs = _mm(cst.cumsum, gb, pb=P_EXACT)
        tots = _mm(cst.chunksum, gb, pb=P_EXACT)
        self.gcs_t = self.gcs.T
        gc = _mm(self.gcs, cst.expand_g, pa=P_EXACT)
        tot = _mm(tots, cst.expand_g, pa=P_EXACT)
        egc = jnp.exp(gc)
        self.q = q
        self.k = k
        self.kb = k * beta
        self.vb = v * beta
        self.kbe = self.kb * egc
        self.qd = q * egc
        self.kd = k * jnp.exp(tot - gc)
        self.gl = jnp.exp(tot)
        self.tt = tt

    def gram(self):
        cst, tt = self.cst, self.tt
        self.qk = []
        lmats = []
        for j in range(N_HEADS // 2):
            sl = _pair_slice(j)
            kb_p, q_p = self.kb[:, sl], self.q[:, sl]
            k_lp = _lowp(self.k[:, sl], P_TIME)
            for half in range(2):
                h = 2 * j + half
                m = cst.half[half]
                diff = self.gcs[:, h:h + 1] - self.gcs_t[h:h + 1, :]
                dec = jnp.where(cst.incl, jnp.exp(jnp.where(cst.incl, diff, 0.0)), 0.0)
                x1 = jnp.concatenate([jnp.where(m, kb_p, 0.0), jnp.where(m, q_p, 0.0)], axis=0)
                gram = _mm(x1, k_lp, _NT, pa=P_TIME, pb=P_TIME)
                a_mat = jnp.where(cst.strict, gram[:tt] * dec, 0.0)
                lmats.append(-a_mat)
                self.qk.append(_lowp(gram[tt:] * dec, P_APPLY))
        return lmats

    def apply(self, t_invs):
        cst = self.cst
        uws = []
        for h, t_inv in enumerate(t_invs):
            sl = _pair_slice(h // 2)
            rhs = jnp.concatenate([self.vb[:, sl], self.kbe[:, sl]], axis=1)
            uws.append(_mm(t_inv, rhs, pa=P_APPLY, pb=P_APPLY))
        per_head = []
        for h, uw in enumerate(uws):
            oq = _mm(self.qk[h], uw, pa=P_APPLY, pb=P_APPLY)
            per_head.append((uw[:, :LANES], uw[:, LANES:], oq[:, :LANES],
                             self.qd[:, _pair_slice(h // 2)] - oq[:, LANES:]))
        self.pairs = _merge_halves(per_head, cst)

    def transitions(self, j, c):
        cst = self.cst
        sl = _pair_slice(j)
        rc = slice(c * CHUNK, (c + 1) * CHUNK)
        u, wm, _, _ = self.pairs[j]
        ku = _mm(self.kd[rc, sl], jnp.concatenate([u[rc], wm[rc]], axis=1), _TN, pa=P_STATE, pb=P_STATE)
        delta = jnp.where(cst.pair_bd, ku[:, :LANES], 0.0)
        phi = jnp.where(cst.pair_eye, self.gl[c * CHUNK:c * CHUNK + 1, sl], 0.0) - jnp.where(
            cst.pair_bd, ku[:, LANES:], 0.0)
        return phi, delta

    def step(self, j, c, s, trans, y_ref, y_off):
        rc = slice(c * CHUNK, (c + 1) * CHUNK)
        _, _, o0, qa = self.pairs[j]
        phi, delta = trans
        y_ref[rc, y_off + j * LANES:y_off + (j + 1) * LANES] = _mm(qa[rc], s, pa=P_STATE, pb=P_STATE) + o0[rc]
        return _mm(phi, s, pa=P_STATE, pb=P_STATE) + delta

    def finish(self, y_ref, y_off):
        cst, w = self.cst, MIX_W
        o = y_ref[:, y_off:y_off + w]
        on = o * lax.rsqrt(_head_sum(o * o, cst) * (1.0 / HEAD_DIM) + NORM_EPS) * self.norm_g
        y_ref[:, y_off:y_off + w] = on * (self.p_z * _sigmoid(self.p_z))


_N_LRU, _N_RWKV, _N_VRES, _N_GDN = 6, 10, 3, 4


def _mixer_body(*refs, has_vres):
    it = iter(refs)
    take = lambda n: [next(it) for _ in range(n)]
    x_ref, mixg_ref, win_ref, wout_ref = take(4)
    lru_prm = [r[...] for r in take(_N_LRU)]
    rwkv_prm = [r[...] for r in take(_N_RWKV)]
    vres = None
    if has_vres:
        vf_ref, = take(1)
        vres = [vf_ref[...]] + [r[...] for r in take(_N_VRES)]
    gdn_prm = [r[...] for r in take(_N_GDN)]
    o_ref, = take(1)
    vout_ref = None
    if not has_vres:
        vout_ref, = take(1)
    lru_carry, lru_h, rw_carry, rw_s, gd_carry, gd_s, y_scr = take(7)

    t_idx = pl.program_id(1)

    @pl.when(t_idx == 0)
    def _():
        for ref in (lru_carry, lru_h, rw_carry, rw_s, gd_carry, gd_s):
            ref[...] = jnp.zeros(ref.shape, ref.dtype)

    cst = _Consts()
    x = x_ref[...]
    h = _rms(x, mixg_ref[...]).astype(BF16)
    proj = _mm(h, win_ref[...])
    o = 0
    p_lx = proj[:, o:o + LRU_W]; o += LRU_W
    p_ly = proj[:, o:o + LRU_W]; o += LRU_W
    rw_in = 3 * MIX_W + 2 * LANES
    p_rw = proj[:, o:o + rw_in]; o += rw_in
    p_qkv = proj[:, o:o + 3 * MIX_W]; o += 3 * MIX_W
    p_z = proj[:, o:o + MIX_W]; o += MIX_W
    p_ab = proj[:, o:o + LANES]
    rw_off, gd_off = LRU_W, LRU_W + MIX_W

    _lru(p_lx, p_ly, lru_prm, lru_carry, lru_h, t_idx, y_scr)
    rw = _Rwkv(p_rw, rwkv_prm, vres, rw_carry, cst)
    if vout_ref is not None:
        vout_ref[...] = rw.v_first
    gd = _Gdn(p_qkv, p_z, p_ab, gdn_prm, gd_carry, cst)
    t_invs = _nil_inverse_many(rw.gram() + gd.gram(), cst.eye, P_INV)
    rw.apply(t_invs[:N_HEADS])
    gd.apply(t_invs[N_HEADS:])
    n_pairs = N_HEADS // 2
    groups = ((rw, rw_s, rw_off), (gd, gd_s, gd_off))
    states = [[s_ref[j] for j in range(n_pairs)] for _, s_ref, _ in groups]
    for c in range(TILE_T // CHUNK):
        trans = [[grp.transitions(j, c) for j in range(n_pairs)] for grp, _, _ in groups]
        for gi, (grp, _, off) in enumerate(groups):
            for j in range(n_pairs):
                states[gi][j] = grp.step(j, c, states[gi][j], trans[gi][j], y_scr, off)
    for gi, (_, s_ref, _) in enumerate(groups):
        for j in range(n_pairs):
            s_ref[j] = states[gi][j]
    rw.finish(y_scr, rw_off)
    gd.finish(y_scr, gd_off)
    o_ref[...] = x + _mm(y_scr[...].astype(BF16), wout_ref[...])


def _mixer_call(x2, n_t, big, lru_prm, rwkv_prm, vres, gdn_prm):
    n, d = x2.shape
    mixg, win, wout = big
    has_vres = vres is not None
    row_map = lambda b, t: (b * n_t + t, 0)
    args = [x2, mixg, win, wout]
    specs = [pl.BlockSpec((TILE_T, d), row_map), _const_spec(mixg.shape), _const_spec(win.shape),
             _const_spec(wout.shape)]

    def add_consts(arrs):
        for a in arrs:
            args.append(a)
            specs.append(_const_spec(a.shape))

    add_consts(lru_prm)
    add_consts(rwkv_prm)
    if has_vres:
        args.append(vres[0])
        specs.append(pl.BlockSpec((TILE_T, MIX_W), row_map))
        add_consts(vres[1:])
    add_consts(gdn_prm)
    out_shape = [jax.ShapeDtypeStruct((n, d), F32)]
    out_specs = [pl.BlockSpec((TILE_T, d), row_map)]
    if not has_vres:
        out_shape.append(jax.ShapeDtypeStruct((n, MIX_W), F32))
        out_specs.append(pl.BlockSpec((TILE_T, MIX_W), row_map))
    rw_in = 3 * MIX_W + 2 * LANES
    scratch = [
        pltpu.VMEM((8, LRU_W), F32), pltpu.VMEM((8, LRU_W), F32),
        pltpu.VMEM((8, rw_in), F32), pltpu.VMEM((N_HEADS // 2, LANES, LANES), F32),
        pltpu.VMEM((8, 3 * MIX_W), F32), pltpu.VMEM((N_HEADS // 2, LANES, LANES), F32),
        pltpu.VMEM((TILE_T, d), F32),
    ]
    res = pl.pallas_call(
        functools.partial(_mixer_body, has_vres=has_vres),
        grid=(n // (n_t * TILE_T), n_t),
        in_specs=specs,
        out_specs=out_specs,
        out_shape=out_shape,
        scratch_shapes=scratch,
        compiler_params=pltpu.CompilerParams(dimension_semantics=("arbitrary", "arbitrary"),
                                             vmem_limit_bytes=VMEM_LIMIT),
        name="mixer_vres" if has_vres else "mixer",
    )(*args)
    return res[0], (vres[0] if has_vres else res[1])


def _row(v):
    return v.reshape(1, -1).astype(F32)


def _pad_lanes(v, width=LANES):
    v = v.reshape(1, -1).astype(F32)
    return jnp.pad(v, ((0, 0), (0, width - v.shape[1])))


def kernel(x, ffn1_norm, ffn1_wi, ffn1_wo, mix_norm, w_in, w_out, lru_conv_w, lru_conv_b, lru_gate_a_w, lru_gate_a_b, lru_gate_x_w, lru_gate_x_b, lru_lambda, lru_out_norm, rwkv_mu, rwkv_w_up, rwkv_w_bias, rwkv_a_up, rwkv_a_bias, rwkv_g_up, rwkv_k_k, rwkv_k_a, rwkv_r_k, rwkv_ln_g, rwkv_ln_b, rwkv_vres_w1, rwkv_vres_w2, rwkv_vres_b, gdn_conv_w, gdn_a_log, gdn_dt_bias, gdn_norm, ffn2_norm, ffn2_wi, ffn2_wo, final_norm):
    bsz, seq, d = x.shape
    depth = ffn1_norm.shape[0]
    assert seq % TILE_T == 0 and d == LRU_W + 2 * MIX_W
    n_t = seq // TILE_T
    x2 = x.reshape(bsz * seq, d)
    d_in = w_in.shape[-1]
    d_in_pad = -(-d_in // LANES) * LANES
    lora = rwkv_w_up.shape[1]
    assert 2 * lora == LANES and rwkv_a_up.shape[1] == lora and rwkv_g_up.shape[1] == LANES
    v_first = None
    for l in range(depth):
        x2 = _ffn_call(x2, ffn1_norm[l], ffn1_wi[l].astype(BF16), ffn1_wo[l].astype(BF16))
        big = (_row(mix_norm[l]), jnp.pad(w_in[l], ((0, 0), (0, d_in_pad - d_in))).astype(BF16), w_out[l].astype(BF16))
        gate_w = jnp.concatenate([jax.scipy.linalg.block_diag(*lru_gate_a_w[l]),
                                  jax.scipy.linalg.block_diag(*lru_gate_x_w[l])], axis=1).astype(BF16)
        gate_b = jnp.concatenate([lru_gate_a_b[l].reshape(1, -1), lru_gate_x_b[l].reshape(1, -1)], axis=1)
        lru_prm = [lru_conv_w[l], _row(lru_conv_b[l]), gate_w, gate_b, _row(lru_lambda[l]), _row(lru_out_norm[l])]
        wa_up = jnp.zeros((LANES, 2 * MIX_W), F32)
        wa_up = wa_up.at[:lora, :MIX_W].set(rwkv_w_up[l]).at[lora:, MIX_W:].set(rwkv_a_up[l]).astype(BF16)
        rwkv_prm = [_row(rwkv_mu[l]), wa_up, _row(rwkv_w_bias[l]), _row(rwkv_a_bias[l]), rwkv_g_up[l].astype(BF16),
                    _row(rwkv_k_k[l]), _row(rwkv_k_a[l]), _row(rwkv_r_k[l]), _row(rwkv_ln_g[l]), _row(rwkv_ln_b[l])]
        vres = None
        if l > 0:
            mv = rwkv_vres_w1.shape[-1]
            vw1 = jnp.pad(rwkv_vres_w1[l - 1], ((0, 0), (0, LANES - mv))).astype(BF16)
            vw2 = jnp.pad(rwkv_vres_w2[l - 1], ((0, LANES - mv), (0, 0))).astype(BF16)
            vres = [v_first, vw1, vw2, _row(rwkv_vres_b[l - 1])]
        gdn_prm = [gdn_conv_w[l], _pad_lanes(gdn_a_log[l]), _pad_lanes(gdn_dt_bias[l]),
                   _row(jnp.tile(gdn_norm[l], N_HEADS))]
        x2, v_first = _mixer_call(x2, n_t, big, lru_prm, rwkv_prm, vres, gdn_prm)
        last = l == depth - 1
        x2 = _ffn_call(x2, ffn2_norm[l], ffn2_wi[l].astype(BF16), ffn2_wo[l].astype(BF16),
                       gf=final_norm if last else None)
    return x2.reshape(bsz, seq, d)
```

```python
import functools

import jax
import jax.numpy as jnp
from jax import lax
from jax.experimental import pallas as pl
from jax.experimental.pallas import tpu as pltpu

F32 = jnp.float32
BF16 = jnp.bfloat16

NORM_EPS = 1e-6
L2_EPS = 1e-6
RWKV_GN_EPS = 64e-5
LRU_C = 8.0
HEAD_DIM = 64
N_HEADS = 6
MIX_W = N_HEADS * HEAD_DIM
LRU_W = 256
CHUNK = 64
LOG2_CHUNK = 6
TILE_T = 256
SUB_T = 128
LANES = 128
FFN_TILE = 512
FFN_CHUNK = 256
VMEM_LIMIT = 56 * 1024 * 1024

P_TIME = 1
P_INV = 1
P_APPLY = 1
P_STATE = 1
P_EXACT = 2
P_SEG = 2

_NN = (((1,), (0,)), ((), ()))
_NT = (((1,), (1,)), ((), ()))
_TN = (((0,), (0,)), ((), ()))


def _pieces(x, n):
    if x.dtype == BF16:
        return [x]
    out = []
    r = x
    for i in range(n):
        p = r.astype(BF16)
        out.append(p)
        if i + 1 < n:
            r = r - p.astype(F32)
    return out


def _mm(a, b, dims=_NN, pa=1, pb=1):
    ap = _pieces(a, pa)
    bp = _pieces(b, pb)
    order = max(len(ap), len(bp))
    acc = None
    for i, x in enumerate(ap):
        for j, y in enumerate(bp):
            if i + j < order:
                t = lax.dot_general(x, y, dims, preferred_element_type=F32)
                acc = t if acc is None else acc + t
    return acc


def _lowp(x, pieces):
    return x.astype(BF16) if pieces == 1 else x


def _sigmoid(x):
    return 1.0 / (1.0 + jnp.exp(-x))


def _softplus(x):
    return jnp.maximum(x, 0.0) + jnp.log1p(jnp.exp(-jnp.abs(x)))


def _expm1(x):
    t = jnp.tanh(0.5 * x)
    return 2.0 * t / (1.0 - t)


def _gelu_tanh(x):
    return 0.5 * x * (1.0 + jnp.tanh(0.7978845608028654 * (x + 0.044715 * (x * x * x))))


def _rms(x, g, eps=NORM_EPS):
    return x * lax.rsqrt(jnp.mean(x * x, axis=-1, keepdims=True) + eps) * g


def _iota(shape, dim):
    return lax.broadcasted_iota(jnp.int32, shape, dim)


def _ffn_body(*refs, d_ff, final):
    if final:
        x_ref, g_ref, wi_ref, wo_ref, gf_ref, o_ref = refs
    else:
        x_ref, g_ref, wi_ref, wo_ref, o_ref = refs
    x = x_ref[...]
    h = _rms(x, g_ref[...]).astype(BF16)
    acc = jnp.zeros_like(x)
    for c in range(d_ff // FFN_CHUNK):
        lo = c * FFN_CHUNK
        gate = _mm(h, wi_ref[:, lo:lo + FFN_CHUNK])
        up = _mm(h, wi_ref[:, d_ff + lo:d_ff + lo + FFN_CHUNK])
        act = (gate * _sigmoid(gate) * up).astype(BF16)
        acc = acc + _mm(act, wo_ref[lo:lo + FFN_CHUNK, :])
    y = x + 0.5 * acc
    if final:
        y = _rms(y, gf_ref[...])
    o_ref[...] = y


def _const_spec(shape):
    return pl.BlockSpec(shape, lambda *_: (0,) * len(shape), pipeline_mode=pl.Buffered(1))


def _ffn_call(x2, g, wi, wo, gf=None):
    n, d = x2.shape
    d_ff = wo.shape[0]
    assert n % FFN_TILE == 0 and d_ff % FFN_CHUNK == 0
    final = gf is not None
    args = [x2, g.reshape(1, d), wi, wo]
    specs = [pl.BlockSpec((FFN_TILE, d), lambda i: (i, 0)), _const_spec((1, d)),
             _const_spec(wi.shape), _const_spec(wo.shape)]
    if final:
        args.append(gf.reshape(1, d))
        specs.append(_const_spec((1, d)))
    return pl.pallas_call(
        functools.partial(_ffn_body, d_ff=d_ff, final=final),
        grid=(n // FFN_TILE,),
        in_specs=specs,
        out_specs=pl.BlockSpec((FFN_TILE, d), lambda i: (i, 0)),
        out_shape=jax.ShapeDtypeStruct((n, d), F32),
        compiler_params=pltpu.CompilerParams(dimension_semantics=("arbitrary",), vmem_limit_bytes=VMEM_LIMIT),
        name="ffn_final" if final else "ffn",
    )(*args)


class _Consts:
    def __init__(self):
        tt = TILE_T
        row = _iota((tt, tt), 0)
        col = _iota((tt, tt), 1)
        same = (row >> LOG2_CHUNK) == (col >> LOG2_CHUNK)
        self.incl = same & (row >= col)
        self.cumsum = jnp.where(self.incl, 1.0, 0.0).astype(BF16)
        self.chunksum = jnp.where(same, 1.0, 0.0).astype(BF16)
        srow = _iota((SUB_T, SUB_T), 0)
        scol = _iota((SUB_T, SUB_T), 1)
        ssame = (srow >> LOG2_CHUNK) == (scol >> LOG2_CHUNK)
        self.sub_incl = ssame & (srow >= scol)
        self.sub_strict = ssame & (srow > scol)
        self.sub_eye = srow == scol
        hr = _iota((MIX_W, MIX_W), 0)
        hc = _iota((MIX_W, MIX_W), 1)
        self.headsum = jnp.where((hr >> LOG2_CHUNK) == (hc >> LOG2_CHUNK), 1.0, 0.0).astype(BF16)
        pr = _iota((LANES, LANES), 0)
        pc = _iota((LANES, LANES), 1)
        self.pair_bd = (pr >> LOG2_CHUNK) == (pc >> LOG2_CHUNK)
        self.pair_eye = pr == pc
        self.lane = _iota((1, LANES), 1)
        self.half = (self.lane < HEAD_DIM, self.lane >= HEAD_DIM)
        er = _iota((LANES, MIX_W), 0)
        ec = _iota((LANES, MIX_W), 1) >> LOG2_CHUNK
        self.expand_g = jnp.where(er == ec, 1.0, 0.0).astype(BF16)
        self.expand_b = jnp.where(er == ec + N_HEADS, 1.0, 0.0).astype(BF16)


def _head_sum(x, cst):
    return _mm(x, cst.headsum, pa=P_SEG)


def _lru(px, py, prm, carry_ref, h_ref, t_idx, y_ref):
    conv_w, conv_b, gate_w, gate_b, lam, out_g = prm
    tt = px.shape[0]
    xx = jnp.concatenate([carry_ref[...], px], axis=0)
    carry_ref[...] = px[tt - 8:, :]
    xc = conv_w[3:4, :] * px + conv_b
    for j in (1, 2, 3):
        xc = xc + conv_w[3 - j:4 - j, :] * pltpu.roll(xx, j, axis=0)[8:, :]
    gates = _mm(xc.astype(BF16), gate_w) + gate_b
    r = _sigmoid(gates[:, :LRU_W])
    i = _sigmoid(gates[:, LRU_W:])
    log_a = -LRU_C * r * _softplus(-lam)
    a = jnp.exp(log_a)
    mult = jnp.sqrt(-_expm1(2.0 * log_a))
    rid = _iota((tt, LRU_W), 0)
    mult = jnp.where((rid == 0) & (t_idx == 0), 1.0, mult)
    b = mult * i * xc
    d = 1
    while d < tt:
        keep = rid >= d
        a_prev = jnp.where(keep, pltpu.roll(a, d, axis=0), 1.0)
        b_prev = jnp.where(keep, pltpu.roll(b, d, axis=0), 0.0)
        b = a * b_prev + b
        a = a * a_prev
        d *= 2
    h = b + a * h_ref[7:8, :]
    h_ref[...] = h[tt - 8:, :]
    y_ref[:, 0:LRU_W] = _rms(h * _gelu_tanh(py), out_g)


def _nil_inverse_many(lmats, eye, pieces):
    accs = [jnp.where(eye, 1.0, l) for l in lmats]
    powers = list(lmats)
    for _ in range(LOG2_CHUNK - 1):
        if pieces == 1:
            powers = [p.astype(BF16) for p in powers]
        powers = [_mm(p, p, pa=pieces, pb=pieces) for p in powers]
        accs = [a + _mm(a, p, pa=pieces, pb=pieces) for a, p in zip(accs, powers)]
    return accs


def _pair_slice(j):
    return slice(j * LANES, (j + 1) * LANES)


def _sub_rows(tt):
    return [slice(s, s + SUB_T) for s in range(0, tt, SUB_T)]


def _merge_halves(per_head, cst):
    return [tuple(jnp.where(cst.half[0], x0, x1) for x0, x1 in zip(per_head[2 * j], per_head[2 * j + 1]))
            for j in range(N_HEADS // 2)]


class _Rwkv:
    def __init__(self, p, prm, vres, carry_ref, cst):
        mu, wa_up, w_bias, a_bias, g_up, k_k, k_a, r_k, self.ln_g, self.ln_b = prm
        self.cst = cst
        tt = p.shape[0]
        w = MIX_W
        rid = _iota((tt, 1), 0)
        xs = jnp.where(rid == 0, carry_ref[7:8, :], pltpu.roll(p, 1, axis=0))
        carry_ref[...] = p[tt - 8:, :]
        xm = p + (xs - p) * mu
        r = xm[:, 0:w]
        k = xm[:, w:2 * w]
        v = xm[:, 2 * w:3 * w]
        x_lora = xm[:, 3 * w:3 * w + LANES]
        xg = xm[:, 3 * w + LANES:3 * w + 2 * LANES]
        z = jnp.where(cst.half[0], jnp.tanh(x_lora), x_lora)
        lwa = _mm(z.astype(BF16), wa_up)
        w_log = -_softplus(-(w_bias + lwa[:, :w])) - 0.5
        logw = -jnp.exp(w_log)
        a = _sigmoid(a_bias + lwa[:, w:])
        self.g = _mm(_sigmoid(xg).astype(BF16), g_up)
        self.v_first = v
        if vres is not None:
            vf, vw1, vw2, vb = vres
            lo = _mm(_mm(v.astype(BF16), vw1).astype(BF16), vw2)
            v = v + (vf - v) * _sigmoid(vb + lo)
        kk = k * k_k
        kk = kk * lax.rsqrt(_head_sum(kk * kk, cst) + L2_EPS)
        k2 = k * (1.0 + (a - 1.0) * k_a)
        bb = kk * a
        cum = _mm(cst.cumsum, logw, pb=P_EXACT)
        tot = _mm(cst.chunksum, logw, pb=P_EXACT)
        inv_w = jnp.exp(-cum)
        rem = jnp.exp(tot - cum)
        self.wc = jnp.exp(tot)
        self.r_t = r * jnp.exp(cum)
        self.a_t = -kk * jnp.exp(cum - logw)
        self.b_t = bb * inv_w
        self.k_t = k2 * inv_w
        self.b_w = bb * rem
        self.k_w = k2 * rem
        self.v = v
        self.bonus_rk = r * k2 * r_k
        self.tt = tt

    def gram(self):
        cst, tt, st = self.cst, self.tt, SUB_T
        self.m_rb, self.gv = [], []
        lmats = []
        for j in range(N_HEADS // 2):
            sl = _pair_slice(j)
            a_p, r_p, b_p, k_p = (t[:, sl] for t in (self.a_t, self.r_t, self.b_t, self.k_t))
            v_lp = _lowp(self.v[:, sl], P_APPLY)
            x2s = [_lowp(jnp.concatenate([b_p[rs], k_p[rs]], axis=0), P_TIME) for rs in _sub_rows(tt)]
            for half in range(2):
                m = cst.half[half]
                a_m, r_m = jnp.where(m, a_p, 0.0), jnp.where(m, r_p, 0.0)
                for x2, rs in zip(x2s, _sub_rows(tt)):
                    x1 = jnp.concatenate([a_m[rs], r_m[rs]], axis=0)
                    gram = _mm(x1, x2, _NT, pa=P_TIME, pb=P_TIME)
                    lmats.append(jnp.where(cst.sub_strict, gram[:st, :st], 0.0))
                    g_ak = jnp.where(cst.sub_strict, gram[:st, st:], 0.0)
                    self.m_rb.append(_lowp(jnp.where(cst.sub_incl, gram[st:, :st], 0.0), P_APPLY))
                    m_rk = jnp.where(cst.sub_incl, gram[st:, st:], 0.0)
                    self.gv.append(_mm(jnp.concatenate([g_ak, m_rk], axis=0), v_lp[rs], pa=P_APPLY, pb=P_APPLY))
        return lmats

    def apply(self, t_invs):
        cst, tt, st = self.cst, self.tt, SUB_T
        subs = list(_sub_rows(tt))
        uts = []
        for i, t_inv in enumerate(t_invs):
            h, rs = i // len(subs), subs[i % len(subs)]
            a_p = self.a_t[rs, _pair_slice(h // 2)]
            uts.append(_mm(t_inv, jnp.concatenate([self.gv[i][:st], a_p], axis=1), pa=P_APPLY, pb=P_APPLY))
        mms = [_mm(m_rb, ut, pa=P_APPLY, pb=P_APPLY) for m_rb, ut in zip(self.m_rb, uts)]
        per_head = []
        for h in range(N_HEADS):
            idx = range(h * len(subs), (h + 1) * len(subs))
            ut = jnp.concatenate([uts[i] for i in idx], axis=0)
            mm = jnp.concatenate([mms[i] for i in idx], axis=0)
            gv_bot = jnp.concatenate([self.gv[i][st:] for i in idx], axis=0)
            r_p = self.r_t[:, _pair_slice(h // 2)]
            per_head.append((ut[:, :LANES], ut[:, LANES:], mm[:, :LANES] + gv_bot, r_p + mm[:, LANES:]))
        self.pairs = _merge_halves(per_head, cst)

    def transitions(self, j, c):
        cst = self.cst
        sl = _pair_slice(j)
        rc = slice(c * CHUNK, (c + 1) * CHUNK)
        u0, ta, _, _ = self.pairs[j]
        phi = jnp.where(cst.pair_eye, self.wc[c * CHUNK:c * CHUNK + 1, sl], 0.0) + jnp.where(
            cst.pair_bd, _mm(ta[rc], self.b_w[rc, sl], _TN, pa=P_STATE, pb=P_STATE), 0.0)
        delta = jnp.where(cst.pair_bd, _mm(jnp.concatenate([u0[rc], self.v[rc, sl]], axis=0),
                                           jnp.concatenate([self.b_w[rc, sl], self.k_w[rc, sl]], axis=0),
                                           _TN, pa=P_STATE, pb=P_STATE), 0.0)
        return phi, delta

    def step(self, j, c, s, trans, y_ref, y_off):
        rc = slice(c * CHUNK, (c + 1) * CHUNK)
        _, _, y0, ra = self.pairs[j]
        phi, delta = trans
        y_ref[rc, y_off + j * LANES:y_off + (j + 1) * LANES] = (
            _mm(ra[rc], s, _NT, pa=P_STATE, pb=P_STATE) + y0[rc])
        return _mm(s, phi, pa=P_STATE, pb=P_STATE) + delta

    def finish(self, y_ref, y_off):
        cst, w = self.cst, MIX_W
        y = y_ref[:, y_off:y_off + w]
        mean = _head_sum(y, cst) * (1.0 / HEAD_DIM)
        yc = y - mean
        var = _head_sum(yc * yc, cst) * (1.0 / HEAD_DIM)
        yn = yc * lax.rsqrt(var + RWKV_GN_EPS) * self.ln_g + self.ln_b
        bonus = _head_sum(self.bonus_rk, cst) * self.v
        y_ref[:, y_off:y_off + w] = (yn + bonus) * self.g


class _Gdn:
    def __init__(self, p_qkv, p_z, p_ab, prm, carry_ref, cst):
        conv_w, a_log, dt_bias, self.norm_g = prm
        self.cst = cst
        self.p_z = p_z
        tt = p_qkv.shape[0]
        w = MIX_W
        xx = jnp.concatenate([carry_ref[...], p_qkv], axis=0)
        carry_ref[...] = p_qkv[tt - 8:, :]
        conv = conv_w[3:4, :] * p_qkv
        for j in (1, 2, 3):
            conv = conv + conv_w[3 - j:4 - j, :] * pltpu.roll(xx, j, axis=0)[8:, :]
        qkv = conv * _sigmoid(conv)
        q = qkv[:, 0:w]
        k = qkv[:, w:2 * w]
        v = qkv[:, 2 * w:3 * w]
        q = q * lax.rsqrt(_head_sum(q * q, cst) + L2_EPS) * (HEAD_DIM ** -0.5)
        k = k * lax.rsqrt(_head_sum(k * k, cst) + L2_EPS)
        gb = jnp.where(cst.lane < N_HEADS, -jnp.exp(a_log) * _softplus(p_ab + dt_bias), _sigmoid(p_ab))
        beta = _mm(gb, cst.expand_b, pa=P_EXACT)
        self.gcs = _mm(cst.cumsum, gb, pb=P_EXACT)
        tots = _mm(cst.chunksum, gb, pb=P_EXACT)
        self.gcs_t = self.gcs.T
        gc = _mm(self.gcs, cst.expand_g, pa=P_EXACT)
        tot = _mm(tots, cst.expand_g, pa=P_EXACT)
        egc = jnp.exp(gc)
        self.q = q
        self.k = k
        self.kb = k * beta
        self.vb = v * beta
        self.kbe = self.kb * egc
        self.qd = q * egc
        self.kd = k * jnp.exp(tot - gc)
        self.gl = jnp.exp(tot)
        self.tt = tt

    def gram(self):
        cst, tt, st = self.cst, self.tt, SUB_T
        self.qk = []
        lmats = []
        for j in range(N_HEADS // 2):
            sl = _pair_slice(j)
            kb_p, q_p = self.kb[:, sl], self.q[:, sl]
            k_lp = _lowp(self.k[:, sl], P_TIME)
            for half in range(2):
                h = 2 * j + half
                m = cst.half[half]
                kb_m, q_m = jnp.where(m, kb_p, 0.0), jnp.where(m, q_p, 0.0)
                for rs in _sub_rows(tt):
                    diff = self.gcs[rs, h:h + 1] - self.gcs_t[h:h + 1, rs]
                    dec = jnp.where(cst.sub_incl, jnp.exp(jnp.where(cst.sub_incl, diff, 0.0)), 0.0)
                    x1 = jnp.concatenate([kb_m[rs], q_m[rs]], axis=0)
                    gram = _mm(x1, k_lp[rs], _NT, pa=P_TIME, pb=P_TIME)
                    a_mat = jnp.where(cst.sub_strict, gram[:st] * dec, 0.0)
                    lmats.append(-a_mat)
                    self.qk.append(_lowp(gram[st:] * dec, P_APPLY))
        return lmats

    def apply(self, t_invs):
        cst = self.cst
        subs = list(_sub_rows(self.tt))
        uws = []
        for i, t_inv in enumerate(t_invs):
            h, rs = i // len(subs), subs[i % len(subs)]
            sl = _pair_slice(h // 2)
            rhs = jnp.concatenate([self.vb[rs, sl], self.kbe[rs, sl]], axis=1)
            uws.append(_mm(t_inv, rhs, pa=P_APPLY, pb=P_APPLY))
        oqs = [_mm(qk, uw, pa=P_APPLY, pb=P_APPLY) for qk, uw in zip(self.qk, uws)]
        per_head = []
        for h in range(N_HEADS):
            idx = range(h * len(subs), (h + 1) * len(subs))
            uw = jnp.concatenate([uws[i] for i in idx], axis=0)
            oq = jnp.concatenate([oqs[i] for i in idx], axis=0)
            per_head.append((uw[:, :LANES], uw[:, LANES:], oq[:, :LANES],
                             self.qd[:, _pair_slice(h // 2)] - oq[:, LANES:]))
        self.pairs = _merge_halves(per_head, cst)

    def transitions(self, j, c):
        cst = self.cst
        sl = _pair_slice(j)
        rc = slice(c * CHUNK, (c + 1) * CHUNK)
        u, wm, _, _ = self.pairs[j]
        ku = _mm(self.kd[rc, sl], jnp.concatenate([u[rc], wm[rc]], axis=1), _TN, pa=P_STATE, pb=P_STATE)
        delta = jnp.where(cst.pair_bd, ku[:, :LANES], 0.0)
        phi = jnp.where(cst.pair_eye, self.gl[c * CHUNK:c * CHUNK + 1, sl], 0.0) - jnp.where(
            cst.pair_bd, ku[:, LANES:], 0.0)
        return phi, delta

    def step(self, j, c, s, trans, y_ref, y_off):
        rc = slice(c * CHUNK, (c + 1) * CHUNK)
        _, _, o0, qa = self.pairs[j]
        phi, delta = trans
        y_ref[rc, y_off + j * LANES:y_off + (j + 1) * LANES] = _mm(qa[rc], s, pa=P_STATE, pb=P_STATE) + o0[rc]
        return _mm(phi, s, pa=P_STATE, pb=P_STATE) + delta

    def finish(self, y_ref, y_off):
        cst, w = self.cst, MIX_W
        o = y_ref[:, y_off:y_off + w]
        on = o * lax.rsqrt(_head_sum(o * o, cst) * (1.0 / HEAD_DIM) + NORM_EPS) * self.norm_g
        y_ref[:, y_off:y_off + w] = on * (self.p_z * _sigmoid(self.p_z))


_N_LRU, _N_RWKV, _N_VRES, _N_GDN = 6, 10, 3, 4


def _mixer_body(*refs, has_vres):
    it = iter(refs)
    take = lambda n: [next(it) for _ in range(n)]
    x_ref, mixg_ref, win_ref, wout_ref = take(4)
    lru_prm = [r[...] for r in take(_N_LRU)]
    rwkv_prm = [r[...] for r in take(_N_RWKV)]
    vres = None
    if has_vres:
        vf_ref, = take(1)
        vres = [vf_ref[...]] + [r[...] for r in take(_N_VRES)]
    gdn_prm = [r[...] for r in take(_N_GDN)]
    o_ref, = take(1)
    vout_ref = None
    if not has_vres:
        vout_ref, = take(1)
    lru_carry, lru_h, rw_carry, rw_s, gd_carry, gd_s, y_scr = take(7)

    t_idx = pl.program_id(1)

    @pl.when(t_idx == 0)
    def _():
        for ref in (lru_carry, lru_h, rw_carry, rw_s, gd_carry, gd_s):
            ref[...] = jnp.zeros(ref.shape, ref.dtype)

    cst = _Consts()
    x = x_ref[...]
    h = _rms(x, mixg_ref[...]).astype(BF16)
    proj = _mm(h, win_ref[...])
    o = 0
    p_lx = proj[:, o:o + LRU_W]; o += LRU_W
    p_ly = proj[:, o:o + LRU_W]; o += LRU_W
    rw_in = 3 * MIX_W + 2 * LANES
    p_rw = proj[:, o:o + rw_in]; o += rw_in
    p_qkv = proj[:, o:o + 3 * MIX_W]; o += 3 * MIX_W
    p_z = proj[:, o:o + MIX_W]; o += MIX_W
    p_ab = proj[:, o:o + LANES]
    rw_off, gd_off = LRU_W, LRU_W + MIX_W

    _lru(p_lx, p_ly, lru_prm, lru_carry, lru_h, t_idx, y_scr)
    rw = _Rwkv(p_rw, rwkv_prm, vres, rw_carry, cst)
    if vout_ref is not None:
        vout_ref[...] = rw.v_first
    gd = _Gdn(p_qkv, p_z, p_ab, gdn_prm, gd_carry, cst)
    rw_l = rw.gram()
    t_invs = _nil_inverse_many(rw_l + gd.gram(), cst.sub_eye, P_INV)
    rw.apply(t_invs[:len(rw_l)])
    gd.apply(t_invs[len(rw_l):])
    n_pairs = N_HEADS // 2
    groups = ((rw, rw_s, rw_off), (gd, gd_s, gd_off))
    states = [[s_ref[j] for j in range(n_pairs)] for _, s_ref, _ in groups]
    for c in range(TILE_T // CHUNK):
        trans = [[grp.transitions(j, c) for j in range(n_pairs)] for grp, _, _ in groups]
        for gi, (grp, _, off) in enumerate(groups):
            for j in range(n_pairs):
                states[gi][j] = grp.step(j, c, states[gi][j], trans[gi][j], y_scr, off)
    for gi, (_, s_ref, _) in enumerate(groups):
        for j in range(n_pairs):
            s_ref[j] = states[gi][j]
    rw.finish(y_scr, rw_off)
    gd.finish(y_scr, gd_off)
    o_ref[...] = x + _mm(y_scr[...].astype(BF16), wout_ref[...])


def _mixer_call(x2, n_t, big, lru_prm, rwkv_prm, vres, gdn_prm):
    n, d = x2.shape
    mixg, win, wout = big
    has_vres = vres is not None
    row_map = lambda b, t: (b * n_t + t, 0)
    args = [x2, mixg, win, wout]
    specs = [pl.BlockSpec((TILE_T, d), row_map), _const_spec(mixg.shape), _const_spec(win.shape),
             _const_spec(wout.shape)]

    def add_consts(arrs):
        for a in arrs:
            args.append(a)
            specs.append(_const_spec(a.shape))

    add_consts(lru_prm)
    add_consts(rwkv_prm)
    if has_vres:
        args.append(vres[0])
        specs.append(pl.BlockSpec((TILE_T, MIX_W), row_map))
        add_consts(vres[1:])
    add_consts(gdn_prm)
    out_shape = [jax.ShapeDtypeStruct((n, d), F32)]
    out_specs = [pl.BlockSpec((TILE_T, d), row_map)]
    if not has_vres:
        out_shape.append(jax.ShapeDtypeStruct((n, MIX_W), F32))
        out_specs.append(pl.BlockSpec((TILE_T, MIX_W), row_map))
    rw_in = 3 * MIX_W + 2 * LANES
    scratch = [
        pltpu.VMEM((8, LRU_W), F32), pltpu.VMEM((8, LRU_W), F32),
        pltpu.VMEM((8, rw_in), F32), pltpu.VMEM((N_HEADS // 2, LANES, LANES), F32),
        pltpu.VMEM((8, 3 * MIX_W), F32), pltpu.VMEM((N_HEADS // 2, LANES, LANES), F32),
        pltpu.VMEM((TILE_T, d), F32),
    ]
    res = pl.pallas_call(
        functools.partial(_mixer_body, has_vres=has_vres),
        grid=(n // (n_t * TILE_T), n_t),
        in_specs=specs,
        out_specs=out_specs,
        out_shape=out_shape,
        scratch_shapes=scratch,
        compiler_params=pltpu.CompilerParams(dimension_semantics=("arbitrary", "arbitrary"),
                                             vmem_limit_bytes=VMEM_LIMIT),
        name="mixer_vres" if has_vres else "mixer",
    )(*args)
    return res[0], (vres[0] if has_vres else res[1])


def _row(v):
    return v.reshape(1, -1).astype(F32)


def _pad_lanes(v, width=LANES):
    v = v.reshape(1, -1).astype(F32)
    return jnp.pad(v, ((0, 0), (0, width - v.shape[1])))


def kernel(x, ffn1_norm, ffn1_wi, ffn1_wo, mix_norm, w_in, w_out, lru_conv_w, lru_conv_b, lru_gate_a_w, lru_gate_a_b, lru_gate_x_w, lru_gate_x_b, lru_lambda, lru_out_norm, rwkv_mu, rwkv_w_up, rwkv_w_bias, rwkv_a_up, rwkv_a_bias, rwkv_g_up, rwkv_k_k, rwkv_k_a, rwkv_r_k, rwkv_ln_g, rwkv_ln_b, rwkv_vres_w1, rwkv_vres_w2, rwkv_vres_b, gdn_conv_w, gdn_a_log, gdn_dt_bias, gdn_norm, ffn2_norm, ffn2_wi, ffn2_wo, final_norm):
    bsz, seq, d = x.shape
    depth = ffn1_norm.shape[0]
    assert seq % TILE_T == 0 and d == LRU_W + 2 * MIX_W
    n_t = seq // TILE_T
    x2 = x.reshape(bsz * seq, d)
    d_in = w_in.shape[-1]
    d_in_pad = -(-d_in // LANES) * LANES
    lora = rwkv_w_up.shape[1]
    assert 2 * lora == LANES and rwkv_a_up.shape[1] == lora and rwkv_g_up.shape[1] == LANES
    v_first = None
    for l in range(depth):
        x2 = _ffn_call(x2, ffn1_norm[l], ffn1_wi[l].astype(BF16), ffn1_wo[l].astype(BF16))
        big = (_row(mix_norm[l]), jnp.pad(w_in[l], ((0, 0), (0, d_in_pad - d_in))).astype(BF16), w_out[l].astype(BF16))
        gate_w = jnp.concatenate([jax.scipy.linalg.block_diag(*lru_gate_a_w[l]),
                                  jax.scipy.linalg.block_diag(*lru_gate_x_w[l])], axis=1).astype(BF16)
        gate_b = jnp.concatenate([lru_gate_a_b[l].reshape(1, -1), lru_gate_x_b[l].reshape(1, -1)], axis=1)
        lru_prm = [lru_conv_w[l], _row(lru_conv_b[l]), gate_w, gate_b, _row(lru_lambda[l]), _row(lru_out_norm[l])]
        wa_up = jnp.zeros((LANES, 2 * MIX_W), F32)
        wa_up = wa_up.at[:lora, :MIX_W].set(rwkv_w_up[l]).at[lora:, MIX_W:].set(rwkv_a_up[l]).astype(BF16)
        rwkv_prm = [_row(rwkv_mu[l]), wa_up, _row(rwkv_w_bias[l]), _row(rwkv_a_bias[l]), rwkv_g_up[l].astype(BF16),
                    _row(rwkv_k_k[l]), _row(rwkv_k_a[l]), _row(rwkv_r_k[l]), _row(rwkv_ln_g[l]), _row(rwkv_ln_b[l])]
        vres = None
        if l > 0:
            mv = rwkv_vres_w1.shape[-1]
            vw1 = jnp.pad(rwkv_vres_w1[l - 1], ((0, 0), (0, LANES - mv))).astype(BF16)
            vw2 = jnp.pad(rwkv_vres_w2[l - 1], ((0, LANES - mv), (0, 0))).astype(BF16)
            vres = [v_first, vw1, vw2, _row(rwkv_vres_b[l - 1])]
        gdn_prm = [gdn_conv_w[l], _pad_lanes(gdn_a_log[l]), _pad_lanes(gdn_dt_bias[l]),
                   _row(jnp.tile(gdn_norm[l], N_HEADS))]
        x2, v_first = _mixer_call(x2, n_t, big, lru_prm, rwkv_prm, vres, gdn_prm)
        last = l == depth - 1
        x2 = _ffn_call(x2, ffn2_norm[l], ffn2_wi[l].astype(BF16), ffn2_wo[l].astype(BF16),
                       gf=final_norm if last else None)
    return x2.reshape(bsz, seq, d)
```

```python
import functools

import jax
import jax.numpy as jnp
from jax import lax
from jax.experimental import pallas as pl
from jax.experimental.pallas import tpu as pltpu

F32 = jnp.float32
BF16 = jnp.bfloat16

NORM_EPS = 1e-6
L2_EPS = 1e-6
RWKV_GN_EPS = 64e-5
LRU_C = 8.0
HEAD_DIM = 64
N_HEADS = 6
MIX_W = N_HEADS * HEAD_DIM
LRU_W = 256
CHUNK = 64
LOG2_CHUNK = 6
TILE_T = 256
SUB_T = 128
LANES = 128
FFN_TILE = 512
FFN_CHUNK = 256
VMEM_LIMIT = 56 * 1024 * 1024

P_TIME = 1
P_INV = 1
P_APPLY = 1
P_STATE = 1
P_EXACT = 2
P_SEG = 1

_NN = (((1,), (0,)), ((), ()))
_NT = (((1,), (1,)), ((), ()))
_TN = (((0,), (0,)), ((), ()))


def _pieces(x, n):
    if x.dtype == BF16:
        return [x]
    out = []
    r = x
    for i in range(n):
        p = r.astype(BF16)
        out.append(p)
        if i + 1 < n:
            r = r - p.astype(F32)
    return out


def _mm(a, b, dims=_NN, pa=1, pb=1):
    ap = _pieces(a, pa)
    bp = _pieces(b, pb)
    order = max(len(ap), len(bp))
    acc = None
    for i, x in enumerate(ap):
        for j, y in enumerate(bp):
            if i + j < order:
                t = lax.dot_general(x, y, dims, preferred_element_type=F32)
                acc = t if acc is None else acc + t
    return acc


def _lowp(x, pieces):
    return x.astype(BF16) if pieces == 1 else x


def _sigmoid(x):
    return 1.0 / (1.0 + jnp.exp(-x))


def _softplus(x):
    return jnp.maximum(x, 0.0) + jnp.log1p(jnp.exp(-jnp.abs(x)))


def _expm1(x):
    t = jnp.tanh(0.5 * x)
    return 2.0 * t / (1.0 - t)


def _gelu_tanh(x):
    return 0.5 * x * (1.0 + jnp.tanh(0.7978845608028654 * (x + 0.044715 * (x * x * x))))


def _rms(x, g, eps=NORM_EPS):
    return x * lax.rsqrt(jnp.mean(x * x, axis=-1, keepdims=True) + eps) * g


def _iota(shape, dim):
    return lax.broadcasted_iota(jnp.int32, shape, dim)


def _ffn_body(*refs, d_ff, final):
    if final:
        x_ref, g_ref, wi_ref, wo_ref, gf_ref, o_ref = refs
    else:
        x_ref, g_ref, wi_ref, wo_ref, o_ref = refs
    x = x_ref[...]
    h = _rms(x, g_ref[...]).astype(BF16)
    acc = jnp.zeros_like(x)
    for c in range(d_ff // FFN_CHUNK):
        lo = c * FFN_CHUNK
        gate = _mm(h, wi_ref[:, lo:lo + FFN_CHUNK])
        up = _mm(h, wi_ref[:, d_ff + lo:d_ff + lo + FFN_CHUNK])
        act = (gate * _sigmoid(gate) * up).astype(BF16)
        acc = acc + _mm(act, wo_ref[lo:lo + FFN_CHUNK, :])
    y = x + 0.5 * acc
    if final:
        y = _rms(y, gf_ref[...])
    o_ref[...] = y


def _const_spec(shape):
    return pl.BlockSpec(shape, lambda *_: (0,) * len(shape), pipeline_mode=pl.Buffered(1))


def _layer_spec(stacked_shape, layer):
    return pl.BlockSpec((None,) + tuple(stacked_shape[1:]), lambda *_: (layer, 0, 0), pipeline_mode=pl.Buffered(1))


def _ffn_call(x2, g, wi, wo, layer, gf=None):
    n, d = x2.shape
    d_ff = wo.shape[1]
    assert n % FFN_TILE == 0 and d_ff % FFN_CHUNK == 0
    final = gf is not None
    args = [x2, g.reshape(1, d), wi, wo]
    specs = [pl.BlockSpec((FFN_TILE, d), lambda i: (i, 0)), _const_spec((1, d)),
             _layer_spec(wi.shape, layer), _layer_spec(wo.shape, layer)]
    if final:
        args.append(gf.reshape(1, d))
        specs.append(_const_spec((1, d)))
    return pl.pallas_call(
        functools.partial(_ffn_body, d_ff=d_ff, final=final),
        grid=(n // FFN_TILE,),
        in_specs=specs,
        out_specs=pl.BlockSpec((FFN_TILE, d), lambda i: (i, 0)),
        out_shape=jax.ShapeDtypeStruct((n, d), F32),
        compiler_params=pltpu.CompilerParams(dimension_semantics=("arbitrary",), vmem_limit_bytes=VMEM_LIMIT),
        name="ffn_final" if final else "ffn",
    )(*args)


class _Consts:
    def __init__(self):
        tt = TILE_T
        row = _iota((tt, tt), 0)
        col = _iota((tt, tt), 1)
        same = (row >> LOG2_CHUNK) == (col >> LOG2_CHUNK)
        self.incl = same & (row >= col)
        self.cumsum = jnp.where(self.incl, 1.0, 0.0).astype(BF16)
        self.chunksum = jnp.where(same, 1.0, 0.0).astype(BF16)
        srow = _iota((SUB_T, SUB_T), 0)
        scol = _iota((SUB_T, SUB_T), 1)
        ssame = (srow >> LOG2_CHUNK) == (scol >> LOG2_CHUNK)
        self.sub_incl = ssame & (srow >= scol)
        self.sub_strict = ssame & (srow > scol)
        self.sub_eye = srow == scol
        hr = _iota((MIX_W, MIX_W), 0)
        hc = _iota((MIX_W, MIX_W), 1)
        self.headsum = jnp.where((hr >> LOG2_CHUNK) == (hc >> LOG2_CHUNK), 1.0, 0.0).astype(BF16)
        pr = _iota((LANES, LANES), 0)
        pc = _iota((LANES, LANES), 1)
        self.pair_bd = (pr >> LOG2_CHUNK) == (pc >> LOG2_CHUNK)
        self.pair_eye = pr == pc
        self.lane = _iota((1, LANES), 1)
        self.half = (self.lane < HEAD_DIM, self.lane >= HEAD_DIM)
        er = _iota((LANES, MIX_W), 0)
        ec = _iota((LANES, MIX_W), 1) >> LOG2_CHUNK
        self.expand_g = jnp.where(er == ec, 1.0, 0.0).astype(BF16)
        self.expand_b = jnp.where(er == ec + N_HEADS, 1.0, 0.0).astype(BF16)


def _head_sum(x, cst):
    return _mm(x, cst.headsum, pa=P_SEG)


def _lru(px, py, prm, carry_ref, h_ref, t_idx, y_ref):
    conv_w, conv_b, gate_w, gate_b, lam, out_g = prm
    tt = px.shape[0]
    xx = jnp.concatenate([carry_ref[...], px], axis=0)
    carry_ref[...] = px[tt - 8:, :]
    xc = conv_w[3:4, :] * px + conv_b
    for j in (1, 2, 3):
        xc = xc + conv_w[3 - j:4 - j, :] * pltpu.roll(xx, j, axis=0)[8:, :]
    gates = _mm(xc.astype(BF16), gate_w) + gate_b
    r = _sigmoid(gates[:, :LRU_W])
    i = _sigmoid(gates[:, LRU_W:])
    log_a = -LRU_C * r * _softplus(-lam)
    a = jnp.exp(log_a)
    mult = jnp.sqrt(-_expm1(2.0 * log_a))
    rid = _iota((tt, LRU_W), 0)
    mult = jnp.where((rid == 0) & (t_idx == 0), 1.0, mult)
    b = mult * i * xc
    d = 1
    while d < tt:
        keep = rid >= d
        a_prev = jnp.where(keep, pltpu.roll(a, d, axis=0), 1.0)
        b_prev = jnp.where(keep, pltpu.roll(b, d, axis=0), 0.0)
        b = a * b_prev + b
        a = a * a_prev
        d *= 2
    h = b + a * h_ref[7:8, :]
    h_ref[...] = h[tt - 8:, :]
    y_ref[:, 0:LRU_W] = _rms(h * _gelu_tanh(py), out_g)


def _nil_inverse_many(lmats, eye, pieces):
    accs = [jnp.where(eye, 1.0, l) for l in lmats]
    powers = list(lmats)
    for _ in range(LOG2_CHUNK - 1):
        if pieces == 1:
            powers = [p.astype(BF16) for p in powers]
        powers = [_mm(p, p, pa=pieces, pb=pieces) for p in powers]
        accs = [a + _mm(a, p, pa=pieces, pb=pieces) for a, p in zip(accs, powers)]
    return accs


def _pair_slice(j):
    return slice(j * LANES, (j + 1) * LANES)


def _sub_rows(tt):
    return [slice(s, s + SUB_T) for s in range(0, tt, SUB_T)]


def _merge_halves(per_head, cst):
    return [tuple(jnp.where(cst.half[0], x0, x1) for x0, x1 in zip(per_head[2 * j], per_head[2 * j + 1]))
            for j in range(N_HEADS // 2)]


class _Rwkv:
    def __init__(self, p, prm, vres, carry_ref, cst):
        mu, wa_up, w_bias, a_bias, g_up, k_k, k_a, r_k, self.ln_g, self.ln_b = prm
        self.cst = cst
        tt = p.shape[0]
        w = MIX_W
        rid = _iota((tt, 1), 0)
        xs = jnp.where(rid == 0, carry_ref[7:8, :], pltpu.roll(p, 1, axis=0))
        carry_ref[...] = p[tt - 8:, :]
        xm = p + (xs - p) * mu
        r = xm[:, 0:w]
        k = xm[:, w:2 * w]
        v = xm[:, 2 * w:3 * w]
        x_lora = xm[:, 3 * w:3 * w + LANES]
        xg = xm[:, 3 * w + LANES:3 * w + 2 * LANES]
        z = jnp.where(cst.half[0], jnp.tanh(x_lora), x_lora)
        lwa = _mm(z.astype(BF16), wa_up)
        w_log = -_softplus(-(w_bias + lwa[:, :w])) - 0.5
        logw = -jnp.exp(w_log)
        a = _sigmoid(a_bias + lwa[:, w:])
        self.g = _mm(_sigmoid(xg).astype(BF16), g_up)
        self.v_first = v
        if vres is not None:
            vf, vw1, vw2, vb = vres
            lo = _mm(_mm(v.astype(BF16), vw1).astype(BF16), vw2)
            v = v + (vf - v) * _sigmoid(vb + lo)
        kk = k * k_k
        kk = kk * lax.rsqrt(_head_sum(kk * kk, cst) + L2_EPS)
        k2 = k * (1.0 + (a - 1.0) * k_a)
        bb = kk * a
        cum = _mm(cst.cumsum, logw, pb=P_EXACT)
        tot = _mm(cst.chunksum, logw, pb=P_EXACT)
        inv_w = jnp.exp(-cum)
        rem = jnp.exp(tot - cum)
        self.wc = jnp.exp(tot)
        self.r_t = r * jnp.exp(cum)
        self.a_t = -kk * jnp.exp(cum - logw)
        self.b_t = bb * inv_w
        self.k_t = k2 * inv_w
        self.b_w = bb * rem
        self.k_w = k2 * rem
        self.v = v
        self.bonus_rk = r * k2 * r_k
        self.tt = tt

    def gram(self):
        cst, tt, st = self.cst, self.tt, SUB_T
        self.m_rb, self.gv = [], []
        lmats = []
        for j in range(N_HEADS // 2):
            sl = _pair_slice(j)
            a_p, r_p, b_p, k_p = (t[:, sl] for t in (self.a_t, self.r_t, self.b_t, self.k_t))
            v_lp = _lowp(self.v[:, sl], P_APPLY)
            x2s = [_lowp(jnp.concatenate([b_p[rs], k_p[rs]], axis=0), P_TIME) for rs in _sub_rows(tt)]
            for half in range(2):
                m = cst.half[half]
                a_m, r_m = jnp.where(m, a_p, 0.0), jnp.where(m, r_p, 0.0)
                for x2, rs in zip(x2s, _sub_rows(tt)):
                    x1 = jnp.concatenate([a_m[rs], r_m[rs]], axis=0)
                    gram = _mm(x1, x2, _NT, pa=P_TIME, pb=P_TIME)
                    lmats.append(jnp.where(cst.sub_strict, gram[:st, :st], 0.0))
                    g_ak = jnp.where(cst.sub_strict, gram[:st, st:], 0.0)
                    self.m_rb.append(_lowp(jnp.where(cst.sub_incl, gram[st:, :st], 0.0), P_APPLY))
                    m_rk = jnp.where(cst.sub_incl, gram[st:, st:], 0.0)
                    self.gv.append(_mm(jnp.concatenate([g_ak, m_rk], axis=0), v_lp[rs], pa=P_APPLY, pb=P_APPLY))
        return lmats

    def apply(self, t_invs):
        cst, tt, st = self.cst, self.tt, SUB_T
        subs = list(_sub_rows(tt))
        uts = []
        for i, t_inv in enumerate(t_invs):
            h, rs = i // len(subs), subs[i % len(subs)]
            a_p = self.a_t[rs, _pair_slice(h // 2)]
            uts.append(_mm(t_inv, jnp.concatenate([self.gv[i][:st], a_p], axis=1), pa=P_APPLY, pb=P_APPLY))
        mms = [_mm(m_rb, ut, pa=P_APPLY, pb=P_APPLY) for m_rb, ut in zip(self.m_rb, uts)]
        per_head = []
        for h in range(N_HEADS):
            idx = range(h * len(subs), (h + 1) * len(subs))
            ut = jnp.concatenate([uts[i] for i in idx], axis=0)
            mm = jnp.concatenate([mms[i] for i in idx], axis=0)
            gv_bot = jnp.concatenate([self.gv[i][st:] for i in idx], axis=0)
            r_p = self.r_t[:, _pair_slice(h // 2)]
            per_head.append((ut[:, :LANES], ut[:, LANES:], mm[:, :LANES] + gv_bot, r_p + mm[:, LANES:]))
        self.pairs = _merge_halves(per_head, cst)

    def transitions(self, j, c):
        cst = self.cst
        sl = _pair_slice(j)
        rc = slice(c * CHUNK, (c + 1) * CHUNK)
        u0, ta, _, _ = self.pairs[j]
        phi = jnp.where(cst.pair_eye, self.wc[c * CHUNK:c * CHUNK + 1, sl], 0.0) + jnp.where(
            cst.pair_bd, _mm(ta[rc], self.b_w[rc, sl], _TN, pa=P_STATE, pb=P_STATE), 0.0)
        delta = jnp.where(cst.pair_bd, _mm(jnp.concatenate([u0[rc], self.v[rc, sl]], axis=0),
                                           jnp.concatenate([self.b_w[rc, sl], self.k_w[rc, sl]], axis=0),
                                           _TN, pa=P_STATE, pb=P_STATE), 0.0)
        return phi, delta

    def step(self, j, c, s, trans, y_ref, y_off):
        rc = slice(c * CHUNK, (c + 1) * CHUNK)
        _, _, y0, ra = self.pairs[j]
        phi, delta = trans
        y_ref[rc, y_off + j * LANES:y_off + (j + 1) * LANES] = (
            _mm(ra[rc], s, _NT, pa=P_STATE, pb=P_STATE) + y0[rc])
        return _mm(s, phi, pa=P_STATE, pb=P_STATE) + delta

    def finish(self, y_ref, y_off):
        cst, w = self.cst, MIX_W
        y = y_ref[:, y_off:y_off + w]
        mean = _head_sum(y, cst) * (1.0 / HEAD_DIM)
        yc = y - mean
        var = _head_sum(yc * yc, cst) * (1.0 / HEAD_DIM)
        yn = yc * lax.rsqrt(var + RWKV_GN_EPS) * self.ln_g + self.ln_b
        bonus = _head_sum(self.bonus_rk, cst) * self.v
        y_ref[:, y_off:y_off + w] = (yn + bonus) * self.g


class _Gdn:
    def __init__(self, p_qkv, p_z, p_ab, prm, carry_ref, cst):
        conv_w, a_log, dt_bias, self.norm_g = prm
        self.cst = cst
        self.p_z = p_z
        tt = p_qkv.shape[0]
        w = MIX_W
        xx = jnp.concatenate([carry_ref[...], p_qkv], axis=0)
        carry_ref[...] = p_qkv[tt - 8:, :]
        conv = conv_w[3:4, :] * p_qkv
        for j in (1, 2, 3):
            conv = conv + conv_w[3 - j:4 - j, :] * pltpu.roll(xx, j, axis=0)[8:, :]
        qkv = conv * _sigmoid(conv)
        q = qkv[:, 0:w]
        k = qkv[:, w:2 * w]
        v = qkv[:, 2 * w:3 * w]
        q = q * lax.rsqrt(_head_sum(q * q, cst) + L2_EPS) * (HEAD_DIM ** -0.5)
        k = k * lax.rsqrt(_head_sum(k * k, cst) + L2_EPS)
        gb = jnp.where(cst.lane < N_HEADS, -jnp.exp(a_log) * _softplus(p_ab + dt_bias), _sigmoid(p_ab))
        beta = _mm(gb, cst.expand_b, pa=P_EXACT)
        self.gcs = _mm(cst.cumsum, gb, pb=P_EXACT)
        tots = _mm(cst.chunksum, gb, pb=P_EXACT)
        self.gcs_t = self.gcs.T
        gc = _mm(self.gcs, cst.expand_g, pa=P_EXACT)
        tot = _mm(tots, cst.expand_g, pa=P_EXACT)
        egc = jnp.exp(gc)
        self.q = q
        self.k = k
        self.kb = k * beta
        self.vb = v * beta
        self.kbe = self.kb * egc
        self.qd = q * egc
        self.kd = k * jnp.exp(tot - gc)
        self.gl = jnp.exp(tot)
        self.tt = tt

    def gram(self):
        cst, tt, st = self.cst, self.tt, SUB_T
        self.qk = []
        lmats = []
        for j in range(N_HEADS // 2):
            sl = _pair_slice(j)
            kb_p, q_p = self.kb[:, sl], self.q[:, sl]
            k_lp = _lowp(self.k[:, sl], P_TIME)
            for half in range(2):
                h = 2 * j + half
                m = cst.half[half]
                kb_m, q_m = jnp.where(m, kb_p, 0.0), jnp.where(m, q_p, 0.0)
                for rs in _sub_rows(tt):
                    diff = self.gcs[rs, h:h + 1] - self.gcs_t[h:h + 1, rs]
                    dec = jnp.where(cst.sub_incl, jnp.exp(jnp.where(cst.sub_incl, diff, 0.0)), 0.0)
                    x1 = jnp.concatenate([kb_m[rs], q_m[rs]], axis=0)
                    gram = _mm(x1, k_lp[rs], _NT, pa=P_TIME, pb=P_TIME)
                    a_mat = jnp.where(cst.sub_strict, gram[:st] * dec, 0.0)
                    lmats.append(-a_mat)
                    self.qk.append(_lowp(gram[st:] * dec, P_APPLY))
        return lmats

    def apply(self, t_invs):
        cst = self.cst
        subs = list(_sub_rows(self.tt))
        uws = []
        for i, t_inv in enumerate(t_invs):
            h, rs = i // len(subs), subs[i % len(subs)]
            sl = _pair_slice(h // 2)
            rhs = jnp.concatenate([self.vb[rs, sl], self.kbe[rs, sl]], axis=1)
            uws.append(_mm(t_inv, rhs, pa=P_APPLY, pb=P_APPLY))
        oqs = [_mm(qk, uw, pa=P_APPLY, pb=P_APPLY) for qk, uw in zip(self.qk, uws)]
        per_head = []
        for h in range(N_HEADS):
            idx = range(h * len(subs), (h + 1) * len(subs))
            uw = jnp.concatenate([uws[i] for i in idx], axis=0)
            oq = jnp.concatenate([oqs[i] for i in idx], axis=0)
            per_head.append((uw[:, :LANES], uw[:, LANES:], oq[:, :LANES],
                             self.qd[:, _pair_slice(h // 2)] - oq[:, LANES:]))
        self.pairs = _merge_halves(per_head, cst)

    def transitions(self, j, c):
        cst = self.cst
        sl = _pair_slice(j)
        rc = slice(c * CHUNK, (c + 1) * CHUNK)
        u, wm, _, _ = self.pairs[j]
        ku = _mm(self.kd[rc, sl], jnp.concatenate([u[rc], wm[rc]], axis=1), _TN, pa=P_STATE, pb=P_STATE)
        delta = jnp.where(cst.pair_bd, ku[:, :LANES], 0.0)
        phi = jnp.where(cst.pair_eye, self.gl[c * CHUNK:c * CHUNK + 1, sl], 0.0) - jnp.where(
            cst.pair_bd, ku[:, LANES:], 0.0)
        return phi, delta

    def step(self, j, c, s, trans, y_ref, y_off):
        rc = slice(c * CHUNK, (c + 1) * CHUNK)
        _, _, o0, qa = self.pairs[j]
        phi, delta = trans
        y_ref[rc, y_off + j * LANES:y_off + (j + 1) * LANES] = _mm(qa[rc], s, pa=P_STATE, pb=P_STATE) + o0[rc]
        return _mm(phi, s, pa=P_STATE, pb=P_STATE) + delta

    def finish(self, y_ref, y_off):
        cst, w = self.cst, MIX_W
        o = y_ref[:, y_off:y_off + w]
        on = o * lax.rsqrt(_head_sum(o * o, cst) * (1.0 / HEAD_DIM) + NORM_EPS) * self.norm_g
        y_ref[:, y_off:y_off + w] = on * (self.p_z * _sigmoid(self.p_z))


_N_LRU, _N_RWKV, _N_VRES, _N_GDN = 6, 10, 3, 4


def _mixer_body(*refs, has_vres):
    it = iter(refs)
    take = lambda n: [next(it) for _ in range(n)]
    x_ref, mixg_ref, win_ref, wout_ref = take(4)
    lru_prm = [r[...] for r in take(_N_LRU)]
    rwkv_prm = [r[...] for r in take(_N_RWKV)]
    vres = None
    if has_vres:
        vf_ref, = take(1)
        vres = [vf_ref[...]] + [r[...] for r in take(_N_VRES)]
    gdn_prm = [r[...] for r in take(_N_GDN)]
    o_ref, = take(1)
    vout_ref = None
    if not has_vres:
        vout_ref, = take(1)
    lru_carry, lru_h, rw_carry, rw_s, gd_carry, gd_s, y_scr = take(7)

    t_idx = pl.program_id(1)

    @pl.when(t_idx == 0)
    def _():
        for ref in (lru_carry, lru_h, rw_carry, rw_s, gd_carry, gd_s):
            ref[...] = jnp.zeros(ref.shape, ref.dtype)

    cst = _Consts()
    x = x_ref[...]
    h = _rms(x, mixg_ref[...]).astype(BF16)
    proj = _mm(h, win_ref[...])
    o = 0
    p_lx = proj[:, o:o + LRU_W]; o += LRU_W
    p_ly = proj[:, o:o + LRU_W]; o += LRU_W
    rw_in = 3 * MIX_W + 2 * LANES
    p_rw = proj[:, o:o + rw_in]; o += rw_in
    p_qkv = proj[:, o:o + 3 * MIX_W]; o += 3 * MIX_W
    p_z = proj[:, o:o + MIX_W]; o += MIX_W
    p_ab = proj[:, o:o + LANES]
    rw_off, gd_off = LRU_W, LRU_W + MIX_W

    _lru(p_lx, p_ly, lru_prm, lru_carry, lru_h, t_idx, y_scr)
    rw = _Rwkv(p_rw, rwkv_prm, vres, rw_carry, cst)
    if vout_ref is not None:
        vout_ref[...] = rw.v_first
    gd = _Gdn(p_qkv, p_z, p_ab, gdn_prm, gd_carry, cst)
    rw_l = rw.gram()
    t_invs = _nil_inverse_many(rw_l + gd.gram(), cst.sub_eye, P_INV)
    rw.apply(t_invs[:len(rw_l)])
    gd.apply(t_invs[len(rw_l):])
    n_pairs = N_HEADS // 2
    groups = ((rw, rw_s, rw_off), (gd, gd_s, gd_off))
    states = [[s_ref[j] for j in range(n_pairs)] for _, s_ref, _ in groups]
    for c in range(TILE_T // CHUNK):
        trans = [[grp.transitions(j, c) for j in range(n_pairs)] for grp, _, _ in groups]
        for gi, (grp, _, off) in enumerate(groups):
            for j in range(n_pairs):
                states[gi][j] = grp.step(j, c, states[gi][j], trans[gi][j], y_scr, off)
    for gi, (_, s_ref, _) in enumerate(groups):
        for j in range(n_pairs):
            s_ref[j] = states[gi][j]
    rw.finish(y_scr, rw_off)
    gd.finish(y_scr, gd_off)
    o_ref[...] = x + _mm(y_scr[...].astype(BF16), wout_ref[...])


def _mixer_call(x2, n_t, big, layer, lru_prm, rwkv_prm, vres, gdn_prm):
    n, d = x2.shape
    mixg, win, wout = big
    has_vres = vres is not None
    row_map = lambda b, t: (b * n_t + t, 0)
    args = [x2, mixg, win, wout]
    specs = [pl.BlockSpec((TILE_T, d), row_map), _const_spec(mixg.shape), _layer_spec(win.shape, layer),
             _layer_spec(wout.shape, layer)]

    def add_consts(arrs):
        for a in arrs:
            args.append(a)
            specs.append(_const_spec(a.shape))

    add_consts(lru_prm)
    add_consts(rwkv_prm)
    if has_vres:
        args.append(vres[0])
        specs.append(pl.BlockSpec((TILE_T, MIX_W), row_map))
        add_consts(vres[1:])
    add_consts(gdn_prm)
    out_shape = [jax.ShapeDtypeStruct((n, d), F32)]
    out_specs = [pl.BlockSpec((TILE_T, d), row_map)]
    if not has_vres:
        out_shape.append(jax.ShapeDtypeStruct((n, MIX_W), F32))
        out_specs.append(pl.BlockSpec((TILE_T, MIX_W), row_map))
    rw_in = 3 * MIX_W + 2 * LANES
    scratch = [
        pltpu.VMEM((8, LRU_W), F32), pltpu.VMEM((8, LRU_W), F32),
        pltpu.VMEM((8, rw_in), F32), pltpu.VMEM((N_HEADS // 2, LANES, LANES), F32),
        pltpu.VMEM((8, 3 * MIX_W), F32), pltpu.VMEM((N_HEADS // 2, LANES, LANES), F32),
        pltpu.VMEM((TILE_T, d), F32),
    ]
    res = pl.pallas_call(
        functools.partial(_mixer_body, has_vres=has_vres),
        grid=(n // (n_t * TILE_T), n_t),
        in_specs=specs,
        out_specs=out_specs,
        out_shape=out_shape,
        scratch_shapes=scratch,
        compiler_params=pltpu.CompilerParams(dimension_semantics=("arbitrary", "arbitrary"),
                                             vmem_limit_bytes=VMEM_LIMIT),
        name="mixer_vres" if has_vres else "mixer",
    )(*args)
    return res[0], (vres[0] if has_vres else res[1])


def _row(v):
    return v.reshape(1, -1).astype(F32)


def _pad_lanes(v, width=LANES):
    v = v.reshape(1, -1).astype(F32)
    return jnp.pad(v, ((0, 0), (0, width - v.shape[1])))


def kernel(x, ffn1_norm, ffn1_wi, ffn1_wo, mix_norm, w_in, w_out, lru_conv_w, lru_conv_b, lru_gate_a_w, lru_gate_a_b, lru_gate_x_w, lru_gate_x_b, lru_lambda, lru_out_norm, rwkv_mu, rwkv_w_up, rwkv_w_bias, rwkv_a_up, rwkv_a_bias, rwkv_g_up, rwkv_k_k, rwkv_k_a, rwkv_r_k, rwkv_ln_g, rwkv_ln_b, rwkv_vres_w1, rwkv_vres_w2, rwkv_vres_b, gdn_conv_w, gdn_a_log, gdn_dt_bias, gdn_norm, ffn2_norm, ffn2_wi, ffn2_wo, final_norm):
    bsz, seq, d = x.shape
    depth = ffn1_norm.shape[0]
    assert seq % TILE_T == 0 and d == LRU_W + 2 * MIX_W
    n_t = seq // TILE_T
    x2 = x.reshape(bsz * seq, d)
    d_in = w_in.shape[-1]
    d_in_pad = -(-d_in // LANES) * LANES
    lora = rwkv_w_up.shape[1]
    assert 2 * lora == LANES and rwkv_a_up.shape[1] == lora and rwkv_g_up.shape[1] == LANES
    v_first = None
    wi1, wo1, wi2, wo2 = (t.astype(BF16) for t in (ffn1_wi, ffn1_wo, ffn2_wi, ffn2_wo))
    win = jnp.pad(w_in.astype(BF16), ((0, 0), (0, 0), (0, d_in_pad - d_in)))
    wout = w_out.astype(BF16)
    for l in range(depth):
        x2 = _ffn_call(x2, ffn1_norm[l], wi1, wo1, l)
        big = (_row(mix_norm[l]), win, wout)
        gate_w = jnp.concatenate([jax.scipy.linalg.block_diag(*lru_gate_a_w[l]),
                                  jax.scipy.linalg.block_diag(*lru_gate_x_w[l])], axis=1).astype(BF16)
        gate_b = jnp.concatenate([lru_gate_a_b[l].reshape(1, -1), lru_gate_x_b[l].reshape(1, -1)], axis=1)
        lru_prm = [lru_conv_w[l], _row(lru_conv_b[l]), gate_w, gate_b, _row(lru_lambda[l]), _row(lru_out_norm[l])]
        wa_up = jnp.zeros((LANES, 2 * MIX_W), F32)
        wa_up = wa_up.at[:lora, :MIX_W].set(rwkv_w_up[l]).at[lora:, MIX_W:].set(rwkv_a_up[l]).astype(BF16)
        rwkv_prm = [_row(rwkv_mu[l]), wa_up, _row(rwkv_w_bias[l]), _row(rwkv_a_bias[l]), rwkv_g_up[l].astype(BF16),
                    _row(rwkv_k_k[l]), _row(rwkv_k_a[l]), _row(rwkv_r_k[l]), _row(rwkv_ln_g[l]), _row(rwkv_ln_b[l])]
        vres = None
        if l > 0:
            mv = rwkv_vres_w1.shape[-1]
            vw1 = jnp.pad(rwkv_vres_w1[l - 1], ((0, 0), (0, LANES - mv))).astype(BF16)
            vw2 = jnp.pad(rwkv_vres_w2[l - 1], ((0, LANES - mv), (0, 0))).astype(BF16)
            vres = [v_first, vw1, vw2, _row(rwkv_vres_b[l - 1])]
        gdn_prm = [gdn_conv_w[l], _pad_lanes(gdn_a_log[l]), _pad_lanes(gdn_dt_bias[l]),
                   _row(jnp.tile(gdn_norm[l], N_HEADS))]
        x2, v_first = _mixer_call(x2, n_t, big, l, lru_prm, rwkv_prm, vres, gdn_prm)
        last = l == depth - 1
        x2 = _ffn_call(x2, ffn2_norm[l], wi2, wo2, l, gf=final_norm if last else None)
    return x2.reshape(bsz, seq, d)
```

```python
import functools

import jax
import jax.numpy as jnp
from jax import lax
from jax.experimental import pallas as pl
from jax.experimental.pallas import tpu as pltpu

F32 = jnp.float32
BF16 = jnp.bfloat16

NORM_EPS = 1e-6
L2_EPS = 1e-6
RWKV_GN_EPS = 64e-5
LRU_C = 8.0
HEAD_DIM = 64
N_HEADS = 6
MIX_W = N_HEADS * HEAD_DIM
LRU_W = 256
CHUNK = 64
LOG2_CHUNK = 6
TILE_T = 256
SUB_T = 128
LANES = 128
FFN_TILE = 512
FFN_CHUNK = 256
VMEM_LIMIT = 56 * 1024 * 1024

P_TIME = 1
P_INV = 1
P_APPLY = 1
P_STATE = 1
P_EXACT = 2
P_SEG = 1

_NN = (((1,), (0,)), ((), ()))
_NT = (((1,), (1,)), ((), ()))
_TN = (((0,), (0,)), ((), ()))


def _pieces(x, n):
    if x.dtype == BF16:
        return [x]
    out = []
    r = x
    for i in range(n):
        p = r.astype(BF16)
        out.append(p)
        if i + 1 < n:
            r = r - p.astype(F32)
    return out


def _mm(a, b, dims=_NN, pa=1, pb=1):
    ap = _pieces(a, pa)
    bp = _pieces(b, pb)
    order = max(len(ap), len(bp))
    acc = None
    for i, x in enumerate(ap):
        for j, y in enumerate(bp):
            if i + j < order:
                t = lax.dot_general(x, y, dims, preferred_element_type=F32)
                acc = t if acc is None else acc + t
    return acc


def _lowp(x, pieces):
    return x.astype(BF16) if pieces == 1 else x


def _sigmoid(x):
    return 1.0 / (1.0 + jnp.exp(-x))


def _softplus(x):
    return jnp.maximum(x, 0.0) + jnp.log1p(jnp.exp(-jnp.abs(x)))


def _expm1(x):
    t = jnp.tanh(0.5 * x)
    return 2.0 * t / (1.0 - t)


def _gelu_tanh(x):
    return 0.5 * x * (1.0 + jnp.tanh(0.7978845608028654 * (x + 0.044715 * (x * x * x))))


def _rms(x, g, eps=NORM_EPS):
    return x * lax.rsqrt(jnp.mean(x * x, axis=-1, keepdims=True) + eps) * g


def _iota(shape, dim):
    return lax.broadcasted_iota(jnp.int32, shape, dim)


def _ffn_body(*refs, d_ff, final):
    if final:
        x_ref, g_ref, wi_ref, wo_ref, gf_ref, o_ref = refs
    else:
        x_ref, g_ref, wi_ref, wo_ref, o_ref = refs
    x = x_ref[...]
    h = _rms(x, g_ref[...]).astype(BF16)
    acc = jnp.zeros_like(x)
    for c in range(d_ff // FFN_CHUNK):
        lo = c * FFN_CHUNK
        gate = _mm(h, wi_ref[:, lo:lo + FFN_CHUNK])
        up = _mm(h, wi_ref[:, d_ff + lo:d_ff + lo + FFN_CHUNK])
        act = (gate * _sigmoid(gate) * up).astype(BF16)
        acc = acc + _mm(act, wo_ref[lo:lo + FFN_CHUNK, :])
    y = x + 0.5 * acc
    if final:
        y = _rms(y, gf_ref[...])
    o_ref[...] = y


def _const_spec(shape):
    return pl.BlockSpec(shape, lambda *_: (0,) * len(shape), pipeline_mode=pl.Buffered(1))


def _layer_spec(stacked_shape, layer):
    return pl.BlockSpec((None,) + tuple(stacked_shape[1:]), lambda *_: (layer, 0, 0), pipeline_mode=pl.Buffered(1))


def _ffn_call(x2, g, wi, wo, layer, gf=None):
    n, d = x2.shape
    d_ff = wo.shape[1]
    assert n % FFN_TILE == 0 and d_ff % FFN_CHUNK == 0
    final = gf is not None
    args = [x2, g.reshape(1, d), wi, wo]
    specs = [pl.BlockSpec((FFN_TILE, d), lambda i: (i, 0)), _const_spec((1, d)),
             _layer_spec(wi.shape, layer), _layer_spec(wo.shape, layer)]
    if final:
        args.append(gf.reshape(1, d))
        specs.append(_const_spec((1, d)))
    return pl.pallas_call(
        functools.partial(_ffn_body, d_ff=d_ff, final=final),
        grid=(n // FFN_TILE,),
        in_specs=specs,
        out_specs=pl.BlockSpec((FFN_TILE, d), lambda i: (i, 0)),
        out_shape=jax.ShapeDtypeStruct((n, d), F32),
        compiler_params=pltpu.CompilerParams(dimension_semantics=("arbitrary",), vmem_limit_bytes=VMEM_LIMIT),
        name="ffn_final" if final else "ffn",
    )(*args)


class _Consts:
    def __init__(self):
        tt = TILE_T
        row = _iota((tt, tt), 0)
        col = _iota((tt, tt), 1)
        same = (row >> LOG2_CHUNK) == (col >> LOG2_CHUNK)
        self.incl = same & (row >= col)
        self.cumsum = jnp.where(self.incl, 1.0, 0.0).astype(BF16)
        self.chunksum = jnp.where(same, 1.0, 0.0).astype(BF16)
        srow = _iota((SUB_T, SUB_T), 0)
        scol = _iota((SUB_T, SUB_T), 1)
        ssame = (srow >> LOG2_CHUNK) == (scol >> LOG2_CHUNK)
        self.sub_incl = ssame & (srow >= scol)
        self.sub_strict = ssame & (srow > scol)
        self.sub_eye = srow == scol
        hr = _iota((MIX_W, MIX_W), 0)
        hc = _iota((MIX_W, MIX_W), 1)
        self.headsum = jnp.where((hr >> LOG2_CHUNK) == (hc >> LOG2_CHUNK), 1.0, 0.0).astype(BF16)
        pr = _iota((LANES, LANES), 0)
        pc = _iota((LANES, LANES), 1)
        self.pair_bd = (pr >> LOG2_CHUNK) == (pc >> LOG2_CHUNK)
        self.pair_eye = pr == pc
        self.lane = _iota((1, LANES), 1)
        self.half = (self.lane < HEAD_DIM, self.lane >= HEAD_DIM)
        self.half2 = tuple(jnp.concatenate([m, m], axis=1) for m in self.half)
        er = _iota((LANES, MIX_W), 0)
        ec = _iota((LANES, MIX_W), 1) >> LOG2_CHUNK
        self.expand_g = jnp.where(er == ec, 1.0, 0.0).astype(BF16)
        self.expand_b = jnp.where(er == ec + N_HEADS, 1.0, 0.0).astype(BF16)


def _head_sum(x, cst):
    return _mm(x, cst.headsum, pa=P_SEG)


def _lru(px, py, prm, carry_ref, h_ref, t_idx, y_ref):
    conv_w, conv_b, gate_w, gate_b, lam, out_g = prm
    tt = px.shape[0]
    xx = jnp.concatenate([carry_ref[...], px], axis=0)
    carry_ref[...] = px[tt - 8:, :]
    xc = conv_w[3:4, :] * px + conv_b
    for j in (1, 2, 3):
        xc = xc + conv_w[3 - j:4 - j, :] * pltpu.roll(xx, j, axis=0)[8:, :]
    gates = _mm(xc.astype(BF16), gate_w) + gate_b
    r = _sigmoid(gates[:, :LRU_W])
    i = _sigmoid(gates[:, LRU_W:])
    log_a = -LRU_C * r * _softplus(-lam)
    a = jnp.exp(log_a)
    mult = jnp.sqrt(-_expm1(2.0 * log_a))
    rid = _iota((tt, LRU_W), 0)
    mult = jnp.where((rid == 0) & (t_idx == 0), 1.0, mult)
    b = mult * i * xc
    d = 1
    while d < tt:
        keep = rid >= d
        a_prev = jnp.where(keep, pltpu.roll(a, d, axis=0), 1.0)
        b_prev = jnp.where(keep, pltpu.roll(b, d, axis=0), 0.0)
        b = a * b_prev + b
        a = a * a_prev
        d *= 2
    h = b + a * h_ref[7:8, :]
    h_ref[...] = h[tt - 8:, :]
    y_ref[:, 0:LRU_W] = _rms(h * _gelu_tanh(py), out_g)


def _nil_inverse_many(lmats, eye, pieces):
    accs = [jnp.where(eye, 1.0, l) for l in lmats]
    powers = list(lmats)
    for _ in range(LOG2_CHUNK - 1):
        if pieces == 1:
            powers = [p.astype(BF16) for p in powers]
        powers = [_mm(p, p, pa=pieces, pb=pieces) for p in powers]
        accs = [a + _mm(a, p, pa=pieces, pb=pieces) for a, p in zip(accs, powers)]
    return accs


def _pair_slice(j):
    return slice(j * LANES, (j + 1) * LANES)


def _sub_rows(tt):
    return [slice(s, s + SUB_T) for s in range(0, tt, SUB_T)]


def _stack_heads(x, halves):
    return jnp.concatenate([jnp.where(halves[0], x, 0.0), jnp.where(halves[1], x, 0.0)], axis=0)


class _Rwkv:
    def __init__(self, p, prm, vres, carry_ref, cst):
        mu, wa_up, w_bias, a_bias, g_up, k_k, k_a, r_k, self.ln_g, self.ln_b = prm
        self.cst = cst
        tt = p.shape[0]
        w = MIX_W
        rid = _iota((tt, 1), 0)
        xs = jnp.where(rid == 0, carry_ref[7:8, :], pltpu.roll(p, 1, axis=0))
        carry_ref[...] = p[tt - 8:, :]
        xm = p + (xs - p) * mu
        r = xm[:, 0:w]
        k = xm[:, w:2 * w]
        v = xm[:, 2 * w:3 * w]
        x_lora = xm[:, 3 * w:3 * w + LANES]
        xg = xm[:, 3 * w + LANES:3 * w + 2 * LANES]
        z = jnp.where(cst.half[0], jnp.tanh(x_lora), x_lora)
        lwa = _mm(z.astype(BF16), wa_up)
        w_log = -_softplus(-(w_bias + lwa[:, :w])) - 0.5
        logw = -jnp.exp(w_log)
        a = _sigmoid(a_bias + lwa[:, w:])
        self.g = _mm(_sigmoid(xg).astype(BF16), g_up)
        self.v_first = v
        if vres is not None:
            vf, vw1, vw2, vb = vres
            lo = _mm(_mm(v.astype(BF16), vw1).astype(BF16), vw2)
            v = v + (vf - v) * _sigmoid(vb + lo)
        kk = k * k_k
        kk = kk * lax.rsqrt(_head_sum(kk * kk, cst) + L2_EPS)
        k2 = k * (1.0 + (a - 1.0) * k_a)
        bb = kk * a
        cum = _mm(cst.cumsum, logw, pb=P_EXACT)
        tot = _mm(cst.chunksum, logw, pb=P_EXACT)
        inv_w = jnp.exp(-cum)
        rem = jnp.exp(tot - cum)
        self.wc = jnp.exp(tot)
        self.r_t = r * jnp.exp(cum)
        self.a_t = -kk * jnp.exp(cum - logw)
        self.b_t = bb * inv_w
        self.k_t = k2 * inv_w
        self.b_w = bb * rem
        self.k_w = k2 * rem
        self.v = v
        self.bonus_rk = r * k2 * r_k
        self.tt = tt

    def gram(self):
        cst, tt, st = self.cst, self.tt, SUB_T
        subs = _sub_rows(tt)
        self.m_rb, self.gv = [], []
        lmats = []
        for j in range(N_HEADS // 2):
            sl = _pair_slice(j)
            a_p, r_p, b_p, k_p = (t[:, sl] for t in (self.a_t, self.r_t, self.b_t, self.k_t))
            x2s = [_lowp(jnp.concatenate([b_p[rs], k_p[rs]], axis=0), P_TIME) for rs in subs]
            blocks = {}
            for half in range(2):
                m = cst.half[half]
                a_m, r_m = jnp.where(m, a_p, 0.0), jnp.where(m, r_p, 0.0)
                for si, (x2, rs) in enumerate(zip(x2s, subs)):
                    x1 = jnp.concatenate([a_m[rs], r_m[rs]], axis=0)
                    gram = _mm(x1, x2, _NT, pa=P_TIME, pb=P_TIME)
                    lmats.append(jnp.where(cst.sub_strict, gram[:st, :st], 0.0))
                    blocks[half, si] = (jnp.where(cst.sub_strict, gram[:st, st:], 0.0),
                                        jnp.where(cst.sub_incl, gram[st:, :st], 0.0),
                                        jnp.where(cst.sub_incl, gram[st:, st:], 0.0))
            for si, rs in enumerate(subs):
                (g_ak0, m_rb0, m_rk0), (g_ak1, m_rb1, m_rk1) = blocks[0, si], blocks[1, si]
                lhs = jnp.concatenate([jnp.concatenate([g_ak0, g_ak1], axis=1),
                                       jnp.concatenate([m_rk0, m_rk1], axis=1)], axis=0)
                self.gv.append(_mm(lhs, _stack_heads(self.v[rs, sl], cst.half), pa=P_APPLY, pb=P_APPLY))
                self.m_rb.append(_lowp(jnp.concatenate([m_rb0, m_rb1], axis=1), P_APPLY))
        return lmats

    def apply(self, t_invs):
        cst, tt, st = self.cst, self.tt, SUB_T
        subs = _sub_rows(tt)
        n_sub = len(subs)
        uts = []
        for j in range(N_HEADS // 2):
            for si, rs in enumerate(subs):
                t_pair = jnp.concatenate([t_invs[(2 * j) * n_sub + si], t_invs[(2 * j + 1) * n_sub + si]], axis=1)
                rhs = jnp.concatenate([self.gv[j * n_sub + si][:st], self.a_t[rs, _pair_slice(j)]], axis=1)
                uts.append(_mm(t_pair, _stack_heads(rhs, cst.half2), pa=P_APPLY, pb=P_APPLY))
        mms = [_mm(m_rb, _stack_heads(ut, cst.half2), pa=P_APPLY, pb=P_APPLY) for m_rb, ut in zip(self.m_rb, uts)]
        self.pairs = []
        for j in range(N_HEADS // 2):
            idx = range(j * n_sub, (j + 1) * n_sub)
            ut = jnp.concatenate([uts[i] for i in idx], axis=0)
            mm = jnp.concatenate([mms[i] for i in idx], axis=0)
            gv_bot = jnp.concatenate([self.gv[i][st:] for i in idx], axis=0)
            r_p = self.r_t[:, _pair_slice(j)]
            self.pairs.append((ut[:, :LANES], ut[:, LANES:], mm[:, :LANES] + gv_bot, r_p + mm[:, LANES:]))

    def transitions(self, j, c):
        cst = self.cst
        sl = _pair_slice(j)
        rc = slice(c * CHUNK, (c + 1) * CHUNK)
        u0, ta, _, _ = self.pairs[j]
        phi = jnp.where(cst.pair_eye, self.wc[c * CHUNK:c * CHUNK + 1, sl], 0.0) + jnp.where(
            cst.pair_bd, _mm(ta[rc], self.b_w[rc, sl], _TN, pa=P_STATE, pb=P_STATE), 0.0)
        delta = jnp.where(cst.pair_bd, _mm(jnp.concatenate([u0[rc], self.v[rc, sl]], axis=0),
                                           jnp.concatenate([self.b_w[rc, sl], self.k_w[rc, sl]], axis=0),
                                           _TN, pa=P_STATE, pb=P_STATE), 0.0)
        return phi, delta

    def step(self, j, c, s, trans, y_ref, y_off):
        rc = slice(c * CHUNK, (c + 1) * CHUNK)
        _, _, y0, ra = self.pairs[j]
        phi, delta = trans
        y_ref[rc, y_off + j * LANES:y_off + (j + 1) * LANES] = (
            _mm(ra[rc], s, _NT, pa=P_STATE, pb=P_STATE) + y0[rc])
        return _mm(s, phi, pa=P_STATE, pb=P_STATE) + delta

    def finish(self, y_ref, y_off):
        cst, w = self.cst, MIX_W
        y = y_ref[:, y_off:y_off + w]
        mean = _head_sum(y, cst) * (1.0 / HEAD_DIM)
        yc = y - mean
        var = _head_sum(yc * yc, cst) * (1.0 / HEAD_DIM)
        yn = yc * lax.rsqrt(var + RWKV_GN_EPS) * self.ln_g + self.ln_b
        bonus = _head_sum(self.bonus_rk, cst) * self.v
        y_ref[:, y_off:y_off + w] = (yn + bonus) * self.g


class _Gdn:
    def __init__(self, p_qkv, p_z, p_ab, prm, carry_ref, cst):
        conv_w, a_log, dt_bias, self.norm_g = prm
        self.cst = cst
        self.p_z = p_z
        tt = p_qkv.shape[0]
        w = MIX_W
        xx = jnp.concatenate([carry_ref[...], p_qkv], axis=0)
        carry_ref[...] = p_qkv[tt - 8:, :]
        conv = conv_w[3:4, :] * p_qkv
        for j in (1, 2, 3):
            conv = conv + conv_w[3 - j:4 - j, :] * pltpu.roll(xx, j, axis=0)[8:, :]
        qkv = conv * _sigmoid(conv)
        q = qkv[:, 0:w]
        k = qkv[:, w:2 * w]
        v = qkv[:, 2 * w:3 * w]
        q = q * lax.rsqrt(_head_sum(q * q, cst) + L2_EPS) * (HEAD_DIM ** -0.5)
        k = k * lax.rsqrt(_head_sum(k * k, cst) + L2_EPS)
        gb = jnp.where(cst.lane < N_HEADS, -jnp.exp(a_log) * _softplus(p_ab + dt_bias), _sigmoid(p_ab))
        beta = _mm(gb, cst.expand_b, pa=P_EXACT)
        self.gcs = _mm(cst.cumsum, gb, pb=P_EXACT)
        tots = _mm(cst.chunksum, gb, pb=P_EXACT)
        self.gcs_t = self.gcs.T
        gc = _mm(self.gcs, cst.expand_g, pa=P_EXACT)
        tot = _mm(tots, cst.expand_g, pa=P_EXACT)
        egc = jnp.exp(gc)
        self.q = q
        self.k = k
        self.kb = k * beta
        self.vb = v * beta
        self.kbe = self.kb * egc
        self.qd = q * egc
        self.kd = k * jnp.exp(tot - gc)
        self.gl = jnp.exp(tot)
        self.tt = tt

    def gram(self):
        cst, tt, st = self.cst, self.tt, SUB_T
        subs = _sub_rows(tt)
        self.qk = []
        lmats = []
        for j in range(N_HEADS // 2):
            sl = _pair_slice(j)
            kb_p, q_p = self.kb[:, sl], self.q[:, sl]
            k_lp = _lowp(self.k[:, sl], P_TIME)
            qk = {}
            for half in range(2):
                h = 2 * j + half
                m = cst.half[half]
                kb_m, q_m = jnp.where(m, kb_p, 0.0), jnp.where(m, q_p, 0.0)
                for si, rs in enumerate(subs):
                    diff = self.gcs[rs, h:h + 1] - self.gcs_t[h:h + 1, rs]
                    dec = jnp.where(cst.sub_incl, jnp.exp(jnp.where(cst.sub_incl, diff, 0.0)), 0.0)
                    x1 = jnp.concatenate([kb_m[rs], q_m[rs]], axis=0)
                    gram = _mm(x1, k_lp[rs], _NT, pa=P_TIME, pb=P_TIME)
                    a_mat = jnp.where(cst.sub_strict, gram[:st] * dec, 0.0)
                    lmats.append(-a_mat)
                    qk[half, si] = gram[st:] * dec
            for si in range(len(subs)):
                self.qk.append(_lowp(jnp.concatenate([qk[0, si], qk[1, si]], axis=1), P_APPLY))
        return lmats

    def apply(self, t_invs):
        cst = self.cst
        subs = _sub_rows(self.tt)
        n_sub = len(subs)
        uws = []
        for j in range(N_HEADS // 2):
            sl = _pair_slice(j)
            for si, rs in enumerate(subs):
                t_pair = jnp.concatenate([t_invs[(2 * j) * n_sub + si], t_invs[(2 * j + 1) * n_sub + si]], axis=1)
                rhs = jnp.concatenate([self.vb[rs, sl], self.kbe[rs, sl]], axis=1)
                uws.append(_mm(t_pair, _stack_heads(rhs, cst.half2), pa=P_APPLY, pb=P_APPLY))
        oqs = [_mm(qk, _stack_heads(uw, cst.half2), pa=P_APPLY, pb=P_APPLY) for qk, uw in zip(self.qk, uws)]
        self.pairs = []
        for j in range(N_HEADS // 2):
            idx = range(j * n_sub, (j + 1) * n_sub)
            uw = jnp.concatenate([uws[i] for i in idx], axis=0)
            oq = jnp.concatenate([oqs[i] for i in idx], axis=0)
            self.pairs.append((uw[:, :LANES], uw[:, LANES:], oq[:, :LANES], self.qd[:, _pair_slice(j)] - oq[:, LANES:]))

    def transitions(self, j, c):
        cst = self.cst
        sl = _pair_slice(j)
        rc = slice(c * CHUNK, (c + 1) * CHUNK)
        u, wm, _, _ = self.pairs[j]
        ku = _mm(self.kd[rc, sl], jnp.concatenate([u[rc], wm[rc]], axis=1), _TN, pa=P_STATE, pb=P_STATE)
        delta = jnp.where(cst.pair_bd, ku[:, :LANES], 0.0)
        phi = jnp.where(cst.pair_eye, self.gl[c * CHUNK:c * CHUNK + 1, sl], 0.0) - jnp.where(
            cst.pair_bd, ku[:, LANES:], 0.0)
        return phi, delta

    def step(self, j, c, s, trans, y_ref, y_off):
        rc = slice(c * CHUNK, (c + 1) * CHUNK)
        _, _, o0, qa = self.pairs[j]
        phi, delta = trans
        y_ref[rc, y_off + j * LANES:y_off + (j + 1) * LANES] = _mm(qa[rc], s, pa=P_STATE, pb=P_STATE) + o0[rc]
        return _mm(phi, s, pa=P_STATE, pb=P_STATE) + delta

    def finish(self, y_ref, y_off):
        cst, w = self.cst, MIX_W
        o = y_ref[:, y_off:y_off + w]
        on = o * lax.rsqrt(_head_sum(o * o, cst) * (1.0 / HEAD_DIM) + NORM_EPS) * self.norm_g
        y_ref[:, y_off:y_off + w] = on * (self.p_z * _sigmoid(self.p_z))


_N_LRU, _N_RWKV, _N_VRES, _N_GDN = 6, 10, 3, 4


def _mixer_body(*refs, has_vres):
    it = iter(refs)
    take = lambda n: [next(it) for _ in range(n)]
    x_ref, mixg_ref, win_ref, wout_ref = take(4)
    lru_prm = [r[...] for r in take(_N_LRU)]
    rwkv_prm = [r[...] for r in take(_N_RWKV)]
    vres = None
    if has_vres:
        vf_ref, = take(1)
        vres = [vf_ref[...]] + [r[...] for r in take(_N_VRES)]
    gdn_prm = [r[...] for r in take(_N_GDN)]
    o_ref, = take(1)
    vout_ref = None
    if not has_vres:
        vout_ref, = take(1)
    lru_carry, lru_h, rw_carry, rw_s, gd_carry, gd_s, y_scr = take(7)

    t_idx = pl.program_id(1)

    @pl.when(t_idx == 0)
    def _():
        for ref in (lru_carry, lru_h, rw_carry, rw_s, gd_carry, gd_s):
            ref[...] = jnp.zeros(ref.shape, ref.dtype)

    cst = _Consts()
    x = x_ref[...]
    h = _rms(x, mixg_ref[...]).astype(BF16)
    proj = _mm(h, win_ref[...])
    o = 0
    p_lx = proj[:, o:o + LRU_W]; o += LRU_W
    p_ly = proj[:, o:o + LRU_W]; o += LRU_W
    rw_in = 3 * MIX_W + 2 * LANES
    p_rw = proj[:, o:o + rw_in]; o += rw_in
    p_qkv = proj[:, o:o + 3 * MIX_W]; o += 3 * MIX_W
    p_z = proj[:, o:o + MIX_W]; o += MIX_W
    p_ab = proj[:, o:o + LANES]
    rw_off, gd_off = LRU_W, LRU_W + MIX_W

    _lru(p_lx, p_ly, lru_prm, lru_carry, lru_h, t_idx, y_scr)
    rw = _Rwkv(p_rw, rwkv_prm, vres, rw_carry, cst)
    if vout_ref is not None:
        vout_ref[...] = rw.v_first
    gd = _Gdn(p_qkv, p_z, p_ab, gdn_prm, gd_carry, cst)
    rw_l = rw.gram()
    t_invs = _nil_inverse_many(rw_l + gd.gram(), cst.sub_eye, P_INV)
    rw.apply(t_invs[:len(rw_l)])
    gd.apply(t_invs[len(rw_l):])
    n_pairs = N_HEADS // 2
    groups = ((rw, rw_s, rw_off), (gd, gd_s, gd_off))
    states = [[s_ref[j] for j in range(n_pairs)] for _, s_ref, _ in groups]
    for c in range(TILE_T // CHUNK):
        trans = [[grp.transitions(j, c) for j in range(n_pairs)] for grp, _, _ in groups]
        for gi, (grp, _, off) in enumerate(groups):
            for j in range(n_pairs):
                states[gi][j] = grp.step(j, c, states[gi][j], trans[gi][j], y_scr, off)
    for gi, (_, s_ref, _) in enumerate(groups):
        for j in range(n_pairs):
            s_ref[j] = states[gi][j]
    rw.finish(y_scr, rw_off)
    gd.finish(y_scr, gd_off)
    o_ref[...] = x + _mm(y_scr[...].astype(BF16), wout_ref[...])


def _mixer_call(x2, n_t, big, layer, lru_prm, rwkv_prm, vres, gdn_prm):
    n, d = x2.shape
    mixg, win, wout = big
    has_vres = vres is not None
    row_map = lambda b, t: (b * n_t + t, 0)
    args = [x2, mixg, win, wout]
    specs = [pl.BlockSpec((TILE_T, d), row_map), _const_spec(mixg.shape), _layer_spec(win.shape, layer),
             _layer_spec(wout.shape, layer)]

    def add_consts(arrs):
        for a in arrs:
            args.append(a)
            specs.append(_const_spec(a.shape))

    add_consts(lru_prm)
    add_consts(rwkv_prm)
    if has_vres:
        args.append(vres[0])
        specs.append(pl.BlockSpec((TILE_T, MIX_W), row_map))
        add_consts(vres[1:])
    add_consts(gdn_prm)
    out_shape = [jax.ShapeDtypeStruct((n, d), F32)]
    out_specs = [pl.BlockSpec((TILE_T, d), row_map)]
    if not has_vres:
        out_shape.append(jax.ShapeDtypeStruct((n, MIX_W), F32))
        out_specs.append(pl.BlockSpec((TILE_T, MIX_W), row_map))
    rw_in = 3 * MIX_W + 2 * LANES
    scratch = [
        pltpu.VMEM((8, LRU_W), F32), pltpu.VMEM((8, LRU_W), F32),
        pltpu.VMEM((8, rw_in), F32), pltpu.VMEM((N_HEADS // 2, LANES, LANES), F32),
        pltpu.VMEM((8, 3 * MIX_W), F32), pltpu.VMEM((N_HEADS // 2, LANES, LANES), F32),
        pltpu.VMEM((TILE_T, d), F32),
    ]
    res = pl.pallas_call(
        functools.partial(_mixer_body, has_vres=has_vres),
        grid=(n // (n_t * TILE_T), n_t),
        in_specs=specs,
        out_specs=out_specs,
        out_shape=out_shape,
        scratch_shapes=scratch,
        compiler_params=pltpu.CompilerParams(dimension_semantics=("arbitrary", "arbitrary"),
                                             vmem_limit_bytes=VMEM_LIMIT),
        name="mixer_vres" if has_vres else "mixer",
    )(*args)
    return res[0], (vres[0] if has_vres else res[1])


def _row(v):
    return v.reshape(1, -1).astype(F32)


def _pad_lanes(v, width=LANES):
    v = v.reshape(1, -1).astype(F32)
    return jnp.pad(v, ((0, 0), (0, width - v.shape[1])))


def kernel(x, ffn1_norm, ffn1_wi, ffn1_wo, mix_norm, w_in, w_out, lru_conv_w, lru_conv_b, lru_gate_a_w, lru_gate_a_b, lru_gate_x_w, lru_gate_x_b, lru_lambda, lru_out_norm, rwkv_mu, rwkv_w_up, rwkv_w_bias, rwkv_a_up, rwkv_a_bias, rwkv_g_up, rwkv_k_k, rwkv_k_a, rwkv_r_k, rwkv_ln_g, rwkv_ln_b, rwkv_vres_w1, rwkv_vres_w2, rwkv_vres_b, gdn_conv_w, gdn_a_log, gdn_dt_bias, gdn_norm, ffn2_norm, ffn2_wi, ffn2_wo, final_norm):
    bsz, seq, d = x.shape
    depth = ffn1_norm.shape[0]
    assert seq % TILE_T == 0 and d == LRU_W + 2 * MIX_W
    n_t = seq // TILE_T
    x2 = x.reshape(bsz * seq, d)
    d_in = w_in.shape[-1]
    d_in_pad = -(-d_in // LANES) * LANES
    lora = rwkv_w_up.shape[1]
    assert 2 * lora == LANES and rwkv_a_up.shape[1] == lora and rwkv_g_up.shape[1] == LANES
    v_first = None
    wi1, wo1, wi2, wo2 = (t.astype(BF16) for t in (ffn1_wi, ffn1_wo, ffn2_wi, ffn2_wo))
    win = jnp.pad(w_in.astype(BF16), ((0, 0), (0, 0), (0, d_in_pad - d_in)))
    wout = w_out.astype(BF16)
    for l in range(depth):
        x2 = _ffn_call(x2, ffn1_norm[l], wi1, wo1, l)
        big = (_row(mix_norm[l]), win, wout)
        gate_w = jnp.concatenate([jax.scipy.linalg.block_diag(*lru_gate_a_w[l]),
                                  jax.scipy.linalg.block_diag(*lru_gate_x_w[l])], axis=1).astype(BF16)
        gate_b = jnp.concatenate([lru_gate_a_b[l].reshape(1, -1), lru_gate_x_b[l].reshape(1, -1)], axis=1)
        lru_prm = [lru_conv_w[l], _row(lru_conv_b[l]), gate_w, gate_b, _row(lru_lambda[l]), _row(lru_out_norm[l])]
        wa_up = jnp.zeros((LANES, 2 * MIX_W), F32)
        wa_up = wa_up.at[:lora, :MIX_W].set(rwkv_w_up[l]).at[lora:, MIX_W:].set(rwkv_a_up[l]).astype(BF16)
        rwkv_prm = [_row(rwkv_mu[l]), wa_up, _row(rwkv_w_bias[l]), _row(rwkv_a_bias[l]), rwkv_g_up[l].astype(BF16),
                    _row(rwkv_k_k[l]), _row(rwkv_k_a[l]), _row(rwkv_r_k[l]), _row(rwkv_ln_g[l]), _row(rwkv_ln_b[l])]
        vres = None
        if l > 0:
            mv = rwkv_vres_w1.shape[-1]
            vw1 = jnp.pad(rwkv_vres_w1[l - 1], ((0, 0), (0, LANES - mv))).astype(BF16)
            vw2 = jnp.pad(rwkv_vres_w2[l - 1], ((0, LANES - mv), (0, 0))).astype(BF16)
            vres = [v_first, vw1, vw2, _row(rwkv_vres_b[l - 1])]
        gdn_prm = [gdn_conv_w[l], _pad_lanes(gdn_a_log[l]), _pad_lanes(gdn_dt_bias[l]),
                   _row(jnp.tile(gdn_norm[l], N_HEADS))]
        x2, v_first = _mixer_call(x2, n_t, big, l, lru_prm, rwkv_prm, vres, gdn_prm)
        last = l == depth - 1
        x2 = _ffn_call(x2, ffn2_norm[l], wi2, wo2, l, gf=final_norm if last else None)
    return x2.reshape(bsz, seq, d)
```

```python
import functools

import jax
import jax.numpy as jnp
from jax import lax
from jax.experimental import pallas as pl
from jax.experimental.pallas import tpu as pltpu

F32 = jnp.float32
BF16 = jnp.bfloat16

NORM_EPS = 1e-6
L2_EPS = 1e-6
RWKV_GN_EPS = 64e-5
LRU_C = 8.0
HEAD_DIM = 64
N_HEADS = 6
MIX_W = N_HEADS * HEAD_DIM
LRU_W = 256
CHUNK = 64
LOG2_CHUNK = 6
TILE_T = 256
SUB_T = 128
LANES = 128
SUBLANES = 8
CONV_WIDTH = 4
FFN_TILE = 512
FFN_CHUNK = 256
VMEM_LIMIT = 56 * 1024 * 1024

P_TIME = 1
P_INV = 1
P_APPLY = 1
P_STATE = 1
P_EXACT = 2
P_SEG = 1

_NN = (((1,), (0,)), ((), ()))
_NT = (((1,), (1,)), ((), ()))
_TN = (((0,), (0,)), ((), ()))


def _pieces(x, n):
    if x.dtype == BF16:
        return [x]
    out = []
    r = x
    for i in range(n):
        p = r.astype(BF16)
        out.append(p)
        if i + 1 < n:
            r = r - p.astype(F32)
    return out


def _mm(a, b, dims=_NN, pa=1, pb=1):
    ap = _pieces(a, pa)
    bp = _pieces(b, pb)
    order = max(len(ap), len(bp))
    acc = None
    for i, x in enumerate(ap):
        for j, y in enumerate(bp):
            if i + j < order:
                t = lax.dot_general(x, y, dims, preferred_element_type=F32)
                acc = t if acc is None else acc + t
    return acc


def _sum01_left(lhs01, x):
    ps = _pieces(x, P_EXACT)
    return _mm(jnp.concatenate([lhs01] * len(ps), axis=1), jnp.concatenate(ps, axis=0))


def _sum01_right(x, rhs01):
    ps = _pieces(x, P_EXACT)
    return _mm(jnp.concatenate(ps, axis=1), jnp.concatenate([rhs01] * len(ps), axis=0))


def _lowp(x, pieces):
    return x.astype(BF16) if pieces == 1 else x


def _sigmoid(x):
    return 1.0 / (1.0 + jnp.exp(-x))


def _softplus(x):
    return jnp.maximum(x, 0.0) + jnp.log1p(jnp.exp(-jnp.abs(x)))


def _expm1(x):
    t = jnp.tanh(0.5 * x)
    return 2.0 * t / (1.0 - t)


def _gelu_tanh(x):
    return 0.5 * x * (1.0 + jnp.tanh(0.7978845608028654 * (x + 0.044715 * (x * x * x))))


def _rms(x, g, eps=NORM_EPS):
    return x * lax.rsqrt(jnp.mean(x * x, axis=-1, keepdims=True) + eps) * g


def _iota(shape, dim):
    return lax.broadcasted_iota(jnp.int32, shape, dim)


def _ffn_body(*refs, d_ff, final):
    if final:
        x_ref, g_ref, wi_ref, wo_ref, gf_ref, o_ref = refs
    else:
        x_ref, g_ref, wi_ref, wo_ref, o_ref = refs
    x = x_ref[...]
    h = _rms(x, g_ref[...]).astype(BF16)
    acc = jnp.zeros_like(x)
    for c in range(d_ff // FFN_CHUNK):
        lo = c * FFN_CHUNK
        gate = _mm(h, wi_ref[:, lo:lo + FFN_CHUNK])
        up = _mm(h, wi_ref[:, d_ff + lo:d_ff + lo + FFN_CHUNK])
        act = (gate * _sigmoid(gate) * up).astype(BF16)
        acc = acc + _mm(act, wo_ref[lo:lo + FFN_CHUNK, :])
    y = x + 0.5 * acc
    if final:
        y = _rms(y, gf_ref[...])
    o_ref[...] = y


def _const_spec(shape):
    return pl.BlockSpec(shape, lambda *_: (0,) * len(shape), pipeline_mode=pl.Buffered(1))


def _layer_spec(stacked_shape, layer):
    return pl.BlockSpec((None,) + tuple(stacked_shape[1:]), lambda *_: (layer, 0, 0), pipeline_mode=pl.Buffered(1))


def _ffn_call(x2, g, wi, wo, layer, gf=None):
    n, d = x2.shape
    d_ff = wo.shape[1]
    assert n % FFN_TILE == 0 and d_ff % FFN_CHUNK == 0
    final = gf is not None
    args = [x2, g.reshape(1, d), wi, wo]
    specs = [pl.BlockSpec((FFN_TILE, d), lambda i: (i, 0)), _const_spec((1, d)),
             _layer_spec(wi.shape, layer), _layer_spec(wo.shape, layer)]
    if final:
        args.append(gf.reshape(1, d))
        specs.append(_const_spec((1, d)))
    return pl.pallas_call(
        functools.partial(_ffn_body, d_ff=d_ff, final=final),
        grid=(n // FFN_TILE,),
        in_specs=specs,
        out_specs=pl.BlockSpec((FFN_TILE, d), lambda i: (i, 0)),
        out_shape=jax.ShapeDtypeStruct((n, d), F32),
        compiler_params=pltpu.CompilerParams(dimension_semantics=("arbitrary",), vmem_limit_bytes=VMEM_LIMIT),
        name="ffn_final" if final else "ffn",
    )(*args)


class _Consts:
    def __init__(self):
        tt = TILE_T
        row = _iota((tt, tt), 0)
        col = _iota((tt, tt), 1)
        same = (row >> LOG2_CHUNK) == (col >> LOG2_CHUNK)
        self.incl = same & (row >= col)
        self.cum_and_total = jnp.concatenate([jnp.where(self.incl, 1.0, 0.0), jnp.where(same, 1.0, 0.0)],
                                             axis=0).astype(BF16)
        srow = _iota((SUB_T, SUB_T), 0)
        scol = _iota((SUB_T, SUB_T), 1)
        ssame = (srow >> LOG2_CHUNK) == (scol >> LOG2_CHUNK)
        self.sub_incl = ssame & (srow >= scol)
        self.sub_strict = ssame & (srow > scol)
        self.sub_eye = srow == scol
        hr = _iota((MIX_W, MIX_W), 0)
        hc = _iota((MIX_W, MIX_W), 1)
        self.headsum = jnp.where((hr >> LOG2_CHUNK) == (hc >> LOG2_CHUNK), 1.0, 0.0).astype(BF16)
        pr = _iota((LANES, LANES), 0)
        pc = _iota((LANES, LANES), 1)
        self.pair_bd = (pr >> LOG2_CHUNK) == (pc >> LOG2_CHUNK)
        self.pair_eye = pr == pc
        self.lane = _iota((1, LANES), 1)
        self.half = (self.lane < HEAD_DIM, self.lane >= HEAD_DIM)
        self.half2 = tuple(jnp.concatenate([m, m], axis=1) for m in self.half)
        er = _iota((LANES, MIX_W), 0)
        ec = _iota((LANES, MIX_W), 1) >> LOG2_CHUNK
        self.expand_g = jnp.where(er == ec, 1.0, 0.0).astype(BF16)
        self.expand_b = jnp.where(er == ec + N_HEADS, 1.0, 0.0).astype(BF16)


def _head_sum(x, cst):
    return _mm(x, cst.headsum, pa=P_SEG)


def _causal_conv(x, conv_w, carry_ref):
    tt = x.shape[0]
    xx = jnp.concatenate([carry_ref[...], x], axis=0)
    carry_ref[...] = x[tt - SUBLANES:, :]
    y = conv_w[CONV_WIDTH - 1:CONV_WIDTH, :] * x
    for j in range(1, CONV_WIDTH):
        y = y + conv_w[CONV_WIDTH - 1 - j:CONV_WIDTH - j, :] * pltpu.roll(xx, j, axis=0)[SUBLANES:, :]
    return y


def _lru(px, py, prm, carry_ref, h_ref, t_idx, y_ref):
    conv_w, conv_b, gate_w, gate_b, lam, out_g = prm
    tt = px.shape[0]
    xc = _causal_conv(px, conv_w, carry_ref) + conv_b
    gates = _mm(xc.astype(BF16), gate_w) + gate_b
    r = _sigmoid(gates[:, :LRU_W])
    i = _sigmoid(gates[:, LRU_W:])
    log_a = -LRU_C * r * _softplus(-lam)
    a = jnp.exp(log_a)
    mult = jnp.sqrt(-_expm1(2.0 * log_a))
    rid = _iota((tt, LRU_W), 0)
    mult = jnp.where((rid == 0) & (t_idx == 0), 1.0, mult)
    b = mult * i * xc
    d = 1
    while d < tt:
        keep = rid >= d
        a_prev = jnp.where(keep, pltpu.roll(a, d, axis=0), 1.0)
        b_prev = jnp.where(keep, pltpu.roll(b, d, axis=0), 0.0)
        b = a * b_prev + b
        a = a * a_prev
        d *= 2
    h = b + a * h_ref[SUBLANES - 1:SUBLANES, :]
    h_ref[...] = h[tt - SUBLANES:, :]
    y_ref[:, 0:LRU_W] = _rms(h * _gelu_tanh(py), out_g)


def _nil_inverse_many(lmats, eye, pieces):
    accs = [jnp.where(eye, 1.0, l) for l in lmats]
    powers = list(lmats)
    for _ in range(LOG2_CHUNK - 1):
        if pieces == 1:
            powers = [p.astype(BF16) for p in powers]
        powers = [_mm(p, p, pa=pieces, pb=pieces) for p in powers]
        accs = [a + _mm(a, p, pa=pieces, pb=pieces) for a, p in zip(accs, powers)]
    return accs


def _pair_slice(j):
    return slice(j * LANES, (j + 1) * LANES)


def _sub_rows(tt):
    return [slice(s, s + SUB_T) for s in range(0, tt, SUB_T)]


def _stack_heads(x, halves):
    return jnp.concatenate([jnp.where(halves[0], x, 0.0), jnp.where(halves[1], x, 0.0)], axis=0)


class _Rwkv:
    def __init__(self, p, prm, vres, carry_ref, cst):
        mu, wa_up, w_bias, a_bias, g_up, k_k, k_a, r_k, self.ln_g, self.ln_b = prm
        self.cst = cst
        tt = p.shape[0]
        w = MIX_W
        rid = _iota((tt, 1), 0)
        xs = jnp.where(rid == 0, carry_ref[SUBLANES - 1:SUBLANES, :], pltpu.roll(p, 1, axis=0))
        carry_ref[...] = p[tt - SUBLANES:, :]
        xm = p + (xs - p) * mu
        r = xm[:, 0:w]
        k = xm[:, w:2 * w]
        v = xm[:, 2 * w:3 * w]
        x_lora = xm[:, 3 * w:3 * w + LANES]
        xg = xm[:, 3 * w + LANES:3 * w + 2 * LANES]
        z = jnp.where(cst.half[0], jnp.tanh(x_lora), x_lora)
        lwa = _mm(z.astype(BF16), wa_up)
        w_log = -_softplus(-(w_bias + lwa[:, :w])) - 0.5
        logw = -jnp.exp(w_log)
        a = _sigmoid(a_bias + lwa[:, w:])
        self.g = _mm(_sigmoid(xg).astype(BF16), g_up)
        self.v_first = v
        if vres is not None:
            vf, vw1, vw2, vb = vres
            lo = _mm(_mm(v.astype(BF16), vw1).astype(BF16), vw2)
            v = v + (vf - v) * _sigmoid(vb + lo)
        kk = k * k_k
        kk = kk * lax.rsqrt(_head_sum(kk * kk, cst) + L2_EPS)
        k2 = k * (1.0 + (a - 1.0) * k_a)
        bb = kk * a
        ct = _sum01_left(cst.cum_and_total, logw)
        cum = ct[:tt]
        tot = ct[tt:]
        inv_w = jnp.exp(-cum)
        rem = jnp.exp(tot - cum)
        self.wc = jnp.exp(tot)
        self.r_t = r * jnp.exp(cum)
        self.a_t = -kk * jnp.exp(cum - logw)
        self.b_t = bb * inv_w
        self.k_t = k2 * inv_w
        self.b_w = bb * rem
        self.k_w = k2 * rem
        self.v = v
        self.bonus_rk = r * k2 * r_k
        self.tt = tt

    def gram(self):
        cst, tt, st = self.cst, self.tt, SUB_T
        subs = _sub_rows(tt)
        self.m_rb, self.gv = [], []
        lmats = []
        for j in range(N_HEADS // 2):
            sl = _pair_slice(j)
            a_p, r_p, b_p, k_p = (t[:, sl] for t in (self.a_t, self.r_t, self.b_t, self.k_t))
            x2s = [_lowp(jnp.concatenate([b_p[rs], k_p[rs]], axis=0), P_TIME) for rs in subs]
            blocks = {}
            for half in range(2):
                m = cst.half[half]
                a_m, r_m = jnp.where(m, a_p, 0.0), jnp.where(m, r_p, 0.0)
                for si, (x2, rs) in enumerate(zip(x2s, subs)):
                    x1 = jnp.concatenate([a_m[rs], r_m[rs]], axis=0)
                    gram = _mm(x1, x2, _NT, pa=P_TIME, pb=P_TIME)
                    lmats.append(jnp.where(cst.sub_strict, gram[:st, :st], 0.0))
                    blocks[half, si] = (jnp.where(cst.sub_strict, gram[:st, st:], 0.0),
                                        jnp.where(cst.sub_incl, gram[st:, :st], 0.0),
                                        jnp.where(cst.sub_incl, gram[st:, st:], 0.0))
            for si, rs in enumerate(subs):
                (g_ak0, m_rb0, m_rk0), (g_ak1, m_rb1, m_rk1) = blocks[0, si], blocks[1, si]
                lhs = jnp.concatenate([jnp.concatenate([g_ak0, g_ak1], axis=1),
                                       jnp.concatenate([m_rk0, m_rk1], axis=1)], axis=0)
                self.gv.append(_mm(lhs, _stack_heads(self.v[rs, sl], cst.half), pa=P_APPLY, pb=P_APPLY))
                self.m_rb.append(_lowp(jnp.concatenate([m_rb0, m_rb1], axis=1), P_APPLY))
        return lmats

    def apply(self, t_invs):
        cst, tt, st = self.cst, self.tt, SUB_T
        subs = _sub_rows(tt)
        n_sub = len(subs)
        uts = []
        for j in range(N_HEADS // 2):
            for si, rs in enumerate(subs):
                t_pair = jnp.concatenate([t_invs[(2 * j) * n_sub + si], t_invs[(2 * j + 1) * n_sub + si]], axis=1)
                rhs = jnp.concatenate([self.gv[j * n_sub + si][:st], self.a_t[rs, _pair_slice(j)]], axis=1)
                uts.append(_mm(t_pair, _stack_heads(rhs, cst.half2), pa=P_APPLY, pb=P_APPLY))
        mms = [_mm(m_rb, _stack_heads(ut, cst.half2), pa=P_APPLY, pb=P_APPLY) for m_rb, ut in zip(self.m_rb, uts)]
        self.pairs = []
        for j in range(N_HEADS // 2):
            idx = range(j * n_sub, (j + 1) * n_sub)
            ut = jnp.concatenate([uts[i] for i in idx], axis=0)
            mm = jnp.concatenate([mms[i] for i in idx], axis=0)
            gv_bot = jnp.concatenate([self.gv[i][st:] for i in idx], axis=0)
            r_p = self.r_t[:, _pair_slice(j)]
            self.pairs.append((ut[:, :LANES], ut[:, LANES:], mm[:, :LANES] + gv_bot, r_p + mm[:, LANES:]))

    def transitions(self, j, c):
        cst = self.cst
        sl = _pair_slice(j)
        rc = slice(c * CHUNK, (c + 1) * CHUNK)
        u0, ta, _, _ = self.pairs[j]
        phi = jnp.where(cst.pair_eye, self.wc[c * CHUNK:c * CHUNK + 1, sl], 0.0) + jnp.where(
            cst.pair_bd, _mm(ta[rc], self.b_w[rc, sl], _TN, pa=P_STATE, pb=P_STATE), 0.0)
        delta = jnp.where(cst.pair_bd, _mm(jnp.concatenate([u0[rc], self.v[rc, sl]], axis=0),
                                           jnp.concatenate([self.b_w[rc, sl], self.k_w[rc, sl]], axis=0),
                                           _TN, pa=P_STATE, pb=P_STATE), 0.0)
        return phi, delta

    def step(self, j, c, s, trans, y_ref, y_off):
        rc = slice(c * CHUNK, (c + 1) * CHUNK)
        _, _, y0, ra = self.pairs[j]
        phi, delta = trans
        y_ref[rc, y_off + j * LANES:y_off + (j + 1) * LANES] = (
            _mm(ra[rc], s, _NT, pa=P_STATE, pb=P_STATE) + y0[rc])
        return _mm(s, phi, pa=P_STATE, pb=P_STATE) + delta

    def finish(self, y_ref, y_off):
        cst, w = self.cst, MIX_W
        y = y_ref[:, y_off:y_off + w]
        mean = _head_sum(y, cst) * (1.0 / HEAD_DIM)
        yc = y - mean
        var = _head_sum(yc * yc, cst) * (1.0 / HEAD_DIM)
        yn = yc * lax.rsqrt(var + RWKV_GN_EPS) * self.ln_g + self.ln_b
        bonus = _head_sum(self.bonus_rk, cst) * self.v
        y_ref[:, y_off:y_off + w] = (yn + bonus) * self.g


class _Gdn:
    def __init__(self, p_qkv, p_z, p_ab, prm, carry_ref, cst):
        conv_w, a_log, dt_bias, self.norm_g = prm
        self.cst = cst
        self.p_z = p_z
        tt = p_qkv.shape[0]
        w = MIX_W
        conv = _causal_conv(p_qkv, conv_w, carry_ref)
        qkv = conv * _sigmoid(conv)
        q = qkv[:, 0:w]
        k = qkv[:, w:2 * w]
        v = qkv[:, 2 * w:3 * w]
        ss = _head_sum(jnp.concatenate([q * q, k * k], axis=0), cst)
        q = q * lax.rsqrt(ss[:tt] + L2_EPS) * (HEAD_DIM ** -0.5)
        k = k * lax.rsqrt(ss[tt:] + L2_EPS)
        gb = jnp.where(cst.lane < N_HEADS, -jnp.exp(a_log) * _softplus(p_ab + dt_bias), _sigmoid(p_ab))
        beta = _sum01_right(gb, cst.expand_b)
        gt = _sum01_left(cst.cum_and_total, gb)
        self.gcs = gt[:tt]
        self.gcs_t = self.gcs.T
        gt_full = _sum01_right(gt, cst.expand_g)
        gc = gt_full[:tt]
        tot = gt_full[tt:]
        egc = jnp.exp(gc)
        self.q = q
        self.k = k
        self.kb = k * beta
        self.vb = v * beta
        self.kbe = self.kb * egc
        self.qd = q * egc
        self.kd = k * jnp.exp(tot - gc)
        self.gl = jnp.exp(tot)
        self.tt = tt

    def gram(self):
        cst, tt, st = self.cst, self.tt, SUB_T
        subs = _sub_rows(tt)
        self.qk = []
        lmats = []
        for j in range(N_HEADS // 2):
            sl = _pair_slice(j)
            kb_p, q_p = self.kb[:, sl], self.q[:, sl]
            k_lp = _lowp(self.k[:, sl], P_TIME)
            qk = {}
            for half in range(2):
                h = 2 * j + half
                m = cst.half[half]
                kb_m, q_m = jnp.where(m, kb_p, 0.0), jnp.where(m, q_p, 0.0)
                for si, rs in enumerate(subs):
                    diff = self.gcs[rs, h:h + 1] - self.gcs_t[h:h + 1, rs]
                    dec = jnp.where(cst.sub_incl, jnp.exp(jnp.where(cst.sub_incl, diff, 0.0)), 0.0)
                    x1 = jnp.concatenate([kb_m[rs], q_m[rs]], axis=0)
                    gram = _mm(x1, k_lp[rs], _NT, pa=P_TIME, pb=P_TIME)
                    a_mat = jnp.where(cst.sub_strict, gram[:st] * dec, 0.0)
                    lmats.append(-a_mat)
                    qk[half, si] = gram[st:] * dec
            for si in range(len(subs)):
                self.qk.append(_lowp(jnp.concatenate([qk[0, si], qk[1, si]], axis=1), P_APPLY))
        return lmats

    def apply(self, t_invs):
        cst = self.cst
        subs = _sub_rows(self.tt)
        n_sub = len(subs)
        uws = []
        for j in range(N_HEADS // 2):
            sl = _pair_slice(j)
            for si, rs in enumerate(subs):
                t_pair = jnp.concatenate([t_invs[(2 * j) * n_sub + si], t_invs[(2 * j + 1) * n_sub + si]], axis=1)
                rhs = jnp.concatenate([self.vb[rs, sl], self.kbe[rs, sl]], axis=1)
                uws.append(_mm(t_pair, _stack_heads(rhs, cst.half2), pa=P_APPLY, pb=P_APPLY))
        oqs = [_mm(qk, _stack_heads(uw, cst.half2), pa=P_APPLY, pb=P_APPLY) for qk, uw in zip(self.qk, uws)]
        self.pairs = []
        for j in range(N_HEADS // 2):
            idx = range(j * n_sub, (j + 1) * n_sub)
            uw = jnp.concatenate([uws[i] for i in idx], axis=0)
            oq = jnp.concatenate([oqs[i] for i in idx], axis=0)
            self.pairs.append((uw[:, :LANES], uw[:, LANES:], oq[:, :LANES], self.qd[:, _pair_slice(j)] - oq[:, LANES:]))

    def transitions(self, j, c):
        cst = self.cst
        sl = _pair_slice(j)
        rc = slice(c * CHUNK, (c + 1) * CHUNK)
        u, wm, _, _ = self.pairs[j]
        ku = _mm(self.kd[rc, sl], jnp.concatenate([u[rc], wm[rc]], axis=1), _TN, pa=P_STATE, pb=P_STATE)
        delta = jnp.where(cst.pair_bd, ku[:, :LANES], 0.0)
        phi = jnp.where(cst.pair_eye, self.gl[c * CHUNK:c * CHUNK + 1, sl], 0.0) - jnp.where(
            cst.pair_bd, ku[:, LANES:], 0.0)
        return phi, delta

    def step(self, j, c, s, trans, y_ref, y_off):
        rc = slice(c * CHUNK, (c + 1) * CHUNK)
        _, _, o0, qa = self.pairs[j]
        phi, delta = trans
        y_ref[rc, y_off + j * LANES:y_off + (j + 1) * LANES] = _mm(qa[rc], s, pa=P_STATE, pb=P_STATE) + o0[rc]
        return _mm(phi, s, pa=P_STATE, pb=P_STATE) + delta

    def finish(self, y_ref, y_off):
        cst, w = self.cst, MIX_W
        o = y_ref[:, y_off:y_off + w]
        on = o * lax.rsqrt(_head_sum(o * o, cst) * (1.0 / HEAD_DIM) + NORM_EPS) * self.norm_g
        y_ref[:, y_off:y_off + w] = on * (self.p_z * _sigmoid(self.p_z))


_N_LRU, _N_RWKV, _N_VRES, _N_GDN = 6, 10, 3, 4


def _mixer_body(*refs, has_vres):
    it = iter(refs)
    take = lambda n: [next(it) for _ in range(n)]
    x_ref, mixg_ref, win_ref, wout_ref = take(4)
    lru_prm = [r[...] for r in take(_N_LRU)]
    rwkv_prm = [r[...] for r in take(_N_RWKV)]
    vres = None
    if has_vres:
        vf_ref, = take(1)
        vres = [vf_ref[...]] + [r[...] for r in take(_N_VRES)]
    gdn_prm = [r[...] for r in take(_N_GDN)]
    o_ref, = take(1)
    vout_ref = None
    if not has_vres:
        vout_ref, = take(1)
    lru_carry, lru_h, rw_carry, rw_s, gd_carry, gd_s, y_scr = take(7)

    t_idx = pl.program_id(1)

    @pl.when(t_idx == 0)
    def _():
        for ref in (lru_carry, lru_h, rw_carry, rw_s, gd_carry, gd_s):
            ref[...] = jnp.zeros(ref.shape, ref.dtype)

    cst = _Consts()
    x = x_ref[...]
    h = _rms(x, mixg_ref[...]).astype(BF16)
    proj = _mm(h, win_ref[...])
    o = 0
    p_lx = proj[:, o:o + LRU_W]; o += LRU_W
    p_ly = proj[:, o:o + LRU_W]; o += LRU_W
    rw_in = 3 * MIX_W + 2 * LANES
    p_rw = proj[:, o:o + rw_in]; o += rw_in
    p_qkv = proj[:, o:o + 3 * MIX_W]; o += 3 * MIX_W
    p_z = proj[:, o:o + MIX_W]; o += MIX_W
    p_ab = proj[:, o:o + LANES]
    rw_off, gd_off = LRU_W, LRU_W + MIX_W

    _lru(p_lx, p_ly, lru_prm, lru_carry, lru_h, t_idx, y_scr)
    rw = _Rwkv(p_rw, rwkv_prm, vres, rw_carry, cst)
    if vout_ref is not None:
        vout_ref[...] = rw.v_first
    gd = _Gdn(p_qkv, p_z, p_ab, gdn_prm, gd_carry, cst)
    rw_l = rw.gram()
    t_invs = _nil_inverse_many(rw_l + gd.gram(), cst.sub_eye, P_INV)
    rw.apply(t_invs[:len(rw_l)])
    gd.apply(t_invs[len(rw_l):])
    n_pairs = N_HEADS // 2
    groups = ((rw, rw_s, rw_off), (gd, gd_s, gd_off))
    states = [[s_ref[j] for j in range(n_pairs)] for _, s_ref, _ in groups]
    for c in range(TILE_T // CHUNK):
        trans = [[grp.transitions(j, c) for j in range(n_pairs)] for grp, _, _ in groups]
        for gi, (grp, _, off) in enumerate(groups):
            for j in range(n_pairs):
                states[gi][j] = grp.step(j, c, states[gi][j], trans[gi][j], y_scr, off)
    for gi, (_, s_ref, _) in enumerate(groups):
        for j in range(n_pairs):
            s_ref[j] = states[gi][j]
    rw.finish(y_scr, rw_off)
    gd.finish(y_scr, gd_off)
    o_ref[...] = x + _mm(y_scr[...].astype(BF16), wout_ref[...])


def _mixer_call(x2, n_t, big, layer, lru_prm, rwkv_prm, vres, gdn_prm):
    n, d = x2.shape
    mixg, win, wout = big
    has_vres = vres is not None
    row_map = lambda b, t: (b * n_t + t, 0)
    args = [x2, mixg, win, wout]
    specs = [pl.BlockSpec((TILE_T, d), row_map), _const_spec(mixg.shape), _layer_spec(win.shape, layer),
             _layer_spec(wout.shape, layer)]

    def add_consts(arrs):
        for a in arrs:
            args.append(a)
            specs.append(_const_spec(a.shape))

    add_consts(lru_prm)
    add_consts(rwkv_prm)
    if has_vres:
        args.append(vres[0])
        specs.append(pl.BlockSpec((TILE_T, MIX_W), row_map))
        add_consts(vres[1:])
    add_consts(gdn_prm)
    out_shape = [jax.ShapeDtypeStruct((n, d), F32)]
    out_specs = [pl.BlockSpec((TILE_T, d), row_map)]
    if not has_vres:
        out_shape.append(jax.ShapeDtypeStruct((n, MIX_W), F32))
        out_specs.append(pl.BlockSpec((TILE_T, MIX_W), row_map))
    rw_in = 3 * MIX_W + 2 * LANES
    scratch = [
        pltpu.VMEM((SUBLANES, LRU_W), F32), pltpu.VMEM((SUBLANES, LRU_W), F32),
        pltpu.VMEM((SUBLANES, rw_in), F32), pltpu.VMEM((N_HEADS // 2, LANES, LANES), F32),
        pltpu.VMEM((SUBLANES, 3 * MIX_W), F32), pltpu.VMEM((N_HEADS // 2, LANES, LANES), F32),
        pltpu.VMEM((TILE_T, d), F32),
    ]
    res = pl.pallas_call(
        functools.partial(_mixer_body, has_vres=has_vres),
        grid=(n // (n_t * TILE_T), n_t),
        in_specs=specs,
        out_specs=out_specs,
        out_shape=out_shape,
        scratch_shapes=scratch,
        compiler_params=pltpu.CompilerParams(dimension_semantics=("arbitrary", "arbitrary"),
                                             vmem_limit_bytes=VMEM_LIMIT),
        name="mixer_vres" if has_vres else "mixer",
    )(*args)
    return res[0], (vres[0] if has_vres else res[1])


def _row(v):
    return v.reshape(1, -1).astype(F32)


def _pad_lanes(v, width=LANES):
    v = v.reshape(1, -1).astype(F32)
    return jnp.pad(v, ((0, 0), (0, width - v.shape[1])))


def kernel(x, ffn1_norm, ffn1_wi, ffn1_wo, mix_norm, w_in, w_out, lru_conv_w, lru_conv_b, lru_gate_a_w, lru_gate_a_b, lru_gate_x_w, lru_gate_x_b, lru_lambda, lru_out_norm, rwkv_mu, rwkv_w_up, rwkv_w_bias, rwkv_a_up, rwkv_a_bias, rwkv_g_up, rwkv_k_k, rwkv_k_a, rwkv_r_k, rwkv_ln_g, rwkv_ln_b, rwkv_vres_w1, rwkv_vres_w2, rwkv_vres_b, gdn_conv_w, gdn_a_log, gdn_dt_bias, gdn_norm, ffn2_norm, ffn2_wi, ffn2_wo, final_norm):
    bsz, seq, d = x.shape
    depth = ffn1_norm.shape[0]
    assert seq % TILE_T == 0 and d == LRU_W + 2 * MIX_W
    n_t = seq // TILE_T
    x2 = x.reshape(bsz * seq, d)
    d_in = w_in.shape[-1]
    d_in_pad = -(-d_in // LANES) * LANES
    lora = rwkv_w_up.shape[1]
    assert 2 * lora == LANES and rwkv_a_up.shape[1] == lora and rwkv_g_up.shape[1] == LANES
    v_first = None
    wi1, wo1, wi2, wo2 = (t.astype(BF16) for t in (ffn1_wi, ffn1_wo, ffn2_wi, ffn2_wo))
    win = jnp.pad(w_in.astype(BF16), ((0, 0), (0, 0), (0, d_in_pad - d_in)))
    wout = w_out.astype(BF16)
    for l in range(depth):
        x2 = _ffn_call(x2, ffn1_norm[l], wi1, wo1, l)
        big = (_row(mix_norm[l]), win, wout)
        gate_w = jnp.concatenate([jax.scipy.linalg.block_diag(*lru_gate_a_w[l]),
                                  jax.scipy.linalg.block_diag(*lru_gate_x_w[l])], axis=1).astype(BF16)
        gate_b = jnp.concatenate([lru_gate_a_b[l].reshape(1, -1), lru_gate_x_b[l].reshape(1, -1)], axis=1)
        lru_prm = [lru_conv_w[l], _row(lru_conv_b[l]), gate_w, gate_b, _row(lru_lambda[l]), _row(lru_out_norm[l])]
        wa_up = jnp.zeros((LANES, 2 * MIX_W), F32)
        wa_up = wa_up.at[:lora, :MIX_W].set(rwkv_w_up[l]).at[lora:, MIX_W:].set(rwkv_a_up[l]).astype(BF16)
        rwkv_prm = [_row(rwkv_mu[l]), wa_up, _row(rwkv_w_bias[l]), _row(rwkv_a_bias[l]), rwkv_g_up[l].astype(BF16),
                    _row(rwkv_k_k[l]), _row(rwkv_k_a[l]), _row(rwkv_r_k[l]), _row(rwkv_ln_g[l]), _row(rwkv_ln_b[l])]
        vres = None
        if l > 0:
            mv = rwkv_vres_w1.shape[-1]
            vw1 = jnp.pad(rwkv_vres_w1[l - 1], ((0, 0), (0, LANES - mv))).astype(BF16)
            vw2 = jnp.pad(rwkv_vres_w2[l - 1], ((0, LANES - mv), (0, 0))).astype(BF16)
            vres = [v_first, vw1, vw2, _row(rwkv_vres_b[l - 1])]
        gdn_prm = [gdn_conv_w[l], _pad_lanes(gdn_a_log[l]), _pad_lanes(gdn_dt_bias[l]),
                   _row(jnp.tile(gdn_norm[l], N_HEADS))]
        x2, v_first = _mixer_call(x2, n_t, big, l, lru_prm, rwkv_prm, vres, gdn_prm)
        last = l == depth - 1
        x2 = _ffn_call(x2, ffn2_norm[l], wi2, wo2, l, gf=final_norm if last else None)
    return x2.reshape(bsz, seq, d)
```

```python
import functools

import jax
import jax.numpy as jnp
from jax import lax
from jax.experimental import pallas as pl
from jax.experimental.pallas import tpu as pltpu

F32 = jnp.float32
BF16 = jnp.bfloat16

NORM_EPS = 1e-6
L2_EPS = 1e-6
RWKV_GN_EPS = 64e-5
LRU_C = 8.0
HEAD_DIM = 64
N_HEADS = 6
MIX_W = N_HEADS * HEAD_DIM
LRU_W = 256
CHUNK = 64
LOG2_CHUNK = 6
TILE_T = 256
SUB_T = 128
LANES = 128
SUBLANES = 8
CONV_WIDTH = 4
FFN_TILE = 512
FFN_CHUNK = 256
VMEM_LIMIT = 56 * 1024 * 1024

P_TIME = 1
P_INV = 1
P_APPLY = 1
P_STATE = 1
P_EXACT = 2
P_SEG = 1

_NN = (((1,), (0,)), ((), ()))
_NT = (((1,), (1,)), ((), ()))
_TN = (((0,), (0,)), ((), ()))


def _pieces(x, n):
    if x.dtype == BF16:
        return [x]
    out = []
    r = x
    for i in range(n):
        p = r.astype(BF16)
        out.append(p)
        if i + 1 < n:
            r = r - p.astype(F32)
    return out


def _mm(a, b, dims=_NN, pa=1, pb=1):
    ap = _pieces(a, pa)
    bp = _pieces(b, pb)
    order = max(len(ap), len(bp))
    acc = None
    for i, x in enumerate(ap):
        for j, y in enumerate(bp):
            if i + j < order:
                t = lax.dot_general(x, y, dims, preferred_element_type=F32)
                acc = t if acc is None else acc + t
    return acc


def _sum01_left(lhs01, x):
    ps = _pieces(x, P_EXACT)
    return _mm(jnp.concatenate([lhs01] * len(ps), axis=1), jnp.concatenate(ps, axis=0))


def _sum01_right(x, rhs01):
    ps = _pieces(x, P_EXACT)
    return _mm(jnp.concatenate(ps, axis=1), jnp.concatenate([rhs01] * len(ps), axis=0))


def _lowp(x, pieces):
    return x.astype(BF16) if pieces == 1 else x


def _sigmoid(x):
    return 1.0 / (1.0 + jnp.exp(-x))


def _softplus(x):
    return jnp.maximum(x, 0.0) + jnp.log1p(jnp.exp(-jnp.abs(x)))


def _expm1(x):
    t = jnp.tanh(0.5 * x)
    return 2.0 * t / (1.0 - t)


def _gelu_tanh(x):
    return 0.5 * x * (1.0 + jnp.tanh(0.7978845608028654 * (x + 0.044715 * (x * x * x))))


def _rms(x, g, eps=NORM_EPS):
    return x * lax.rsqrt(jnp.mean(x * x, axis=-1, keepdims=True) + eps) * g


def _iota(shape, dim):
    return lax.broadcasted_iota(jnp.int32, shape, dim)


def _ffn_body(*refs, d_ff, final):
    if final:
        x_ref, g_ref, wi_ref, wo_ref, gf_ref, o_ref = refs
    else:
        x_ref, g_ref, wi_ref, wo_ref, o_ref = refs
    x = x_ref[...]
    h = _rms(x, g_ref[...]).astype(BF16)
    acc = jnp.zeros_like(x)
    for c in range(d_ff // FFN_CHUNK):
        lo = c * FFN_CHUNK
        gate = _mm(h, wi_ref[:, lo:lo + FFN_CHUNK])
        up = _mm(h, wi_ref[:, d_ff + lo:d_ff + lo + FFN_CHUNK])
        act = (gate * _sigmoid(gate) * up).astype(BF16)
        acc = acc + _mm(act, wo_ref[lo:lo + FFN_CHUNK, :])
    y = x + 0.5 * acc
    if final:
        y = _rms(y, gf_ref[...])
    o_ref[...] = y


def _const_spec(shape):
    return pl.BlockSpec(shape, lambda *_: (0,) * len(shape), pipeline_mode=pl.Buffered(1))


def _layer_spec(stacked_shape, layer):
    return pl.BlockSpec((None,) + tuple(stacked_shape[1:]), lambda *_: (layer, 0, 0), pipeline_mode=pl.Buffered(1))


def _ffn_call(x2, g, wi, wo, layer, gf=None):
    n, d = x2.shape
    d_ff = wo.shape[1]
    assert n % FFN_TILE == 0 and d_ff % FFN_CHUNK == 0
    final = gf is not None
    args = [x2, g.reshape(1, d), wi, wo]
    specs = [pl.BlockSpec((FFN_TILE, d), lambda i: (i, 0)), _const_spec((1, d)),
             _layer_spec(wi.shape, layer), _layer_spec(wo.shape, layer)]
    if final:
        args.append(gf.reshape(1, d))
        specs.append(_const_spec((1, d)))
    return pl.pallas_call(
        functools.partial(_ffn_body, d_ff=d_ff, final=final),
        grid=(n // FFN_TILE,),
        in_specs=specs,
        out_specs=pl.BlockSpec((FFN_TILE, d), lambda i: (i, 0)),
        out_shape=jax.ShapeDtypeStruct((n, d), F32),
        compiler_params=pltpu.CompilerParams(dimension_semantics=("arbitrary",), vmem_limit_bytes=VMEM_LIMIT),
        name="ffn_final" if final else "ffn",
    )(*args)


class _Consts:
    def __init__(self):
        tt = TILE_T
        row = _iota((tt, tt), 0)
        col = _iota((tt, tt), 1)
        same = (row >> LOG2_CHUNK) == (col >> LOG2_CHUNK)
        self.incl = same & (row >= col)
        self.cum_and_total = jnp.concatenate([jnp.where(self.incl, 1.0, 0.0), jnp.where(same, 1.0, 0.0)],
                                             axis=0).astype(BF16)
        srow = _iota((SUB_T, SUB_T), 0)
        scol = _iota((SUB_T, SUB_T), 1)
        ssame = (srow >> LOG2_CHUNK) == (scol >> LOG2_CHUNK)
        self.sub_incl = ssame & (srow >= scol)
        self.sub_strict = ssame & (srow > scol)
        self.sub_eye = srow == scol
        hr = _iota((MIX_W, MIX_W), 0)
        hc = _iota((MIX_W, MIX_W), 1)
        self.headsum = jnp.where((hr >> LOG2_CHUNK) == (hc >> LOG2_CHUNK), 1.0, 0.0).astype(BF16)
        pr = _iota((LANES, LANES), 0)
        pc = _iota((LANES, LANES), 1)
        self.pair_bd = (pr >> LOG2_CHUNK) == (pc >> LOG2_CHUNK)
        self.pair_eye = pr == pc
        self.lane = _iota((1, LANES), 1)
        self.half = (self.lane < HEAD_DIM, self.lane >= HEAD_DIM)
        self.half2 = tuple(jnp.concatenate([m, m], axis=1) for m in self.half)
        er = _iota((LANES, MIX_W), 0)
        ec = _iota((LANES, MIX_W), 1) >> LOG2_CHUNK
        self.expand_g = jnp.where(er == ec, 1.0, 0.0).astype(BF16)
        self.expand_b = jnp.where(er == ec + N_HEADS, 1.0, 0.0).astype(BF16)


def _head_sum(x, cst):
    return _mm(x, cst.headsum, pa=P_SEG)


def _causal_conv(x, conv_w, carry_ref):
    tt = x.shape[0]
    xx = jnp.concatenate([carry_ref[...], x], axis=0)
    carry_ref[...] = x[tt - SUBLANES:, :]
    y = conv_w[CONV_WIDTH - 1:CONV_WIDTH, :] * x
    for j in range(1, CONV_WIDTH):
        y = y + conv_w[CONV_WIDTH - 1 - j:CONV_WIDTH - j, :] * pltpu.roll(xx, j, axis=0)[SUBLANES:, :]
    return y


def _lru(px, py, prm, carry_ref, h_ref, t_idx, y_ref):
    conv_w, conv_b, gate_w, gate_b, lam, out_g = prm
    tt = px.shape[0]
    xc = _causal_conv(px, conv_w, carry_ref) + conv_b
    gates = _mm(xc.astype(BF16), gate_w) + gate_b
    r = _sigmoid(gates[:, :LRU_W])
    i = _sigmoid(gates[:, LRU_W:])
    log_a = -LRU_C * r * _softplus(-lam)
    a = jnp.exp(log_a)
    mult = jnp.sqrt(-_expm1(2.0 * log_a))
    rid = _iota((tt, LRU_W), 0)
    mult = jnp.where((rid == 0) & (t_idx == 0), 1.0, mult)
    b = mult * i * xc
    sub = rid & (SUBLANES - 1)
    d = 1
    while d < SUBLANES:
        keep = sub >= d
        a_prev = jnp.where(keep, pltpu.roll(a, d, axis=0), 1.0)
        b_prev = jnp.where(keep, pltpu.roll(b, d, axis=0), 0.0)
        b = a * b_prev + b
        a = a * a_prev
        d *= 2
    carry = h_ref[SUBLANES - 1:SUBLANES, :]
    groups = []
    for g in range(tt // SUBLANES):
        blk = slice(g * SUBLANES, (g + 1) * SUBLANES)
        groups.append(b[blk] + a[blk] * carry)
        carry = groups[-1][SUBLANES - 1:SUBLANES, :]
    h = jnp.concatenate(groups, axis=0)
    h_ref[...] = groups[-1]
    y_ref[:, 0:LRU_W] = _rms(h * _gelu_tanh(py), out_g)


def _nil_inverse_many(lmats, eye, pieces):
    accs = [jnp.where(eye, 1.0, l) for l in lmats]
    powers = list(lmats)
    for _ in range(LOG2_CHUNK - 1):
        if pieces == 1:
            powers = [p.astype(BF16) for p in powers]
        powers = [_mm(p, p, pa=pieces, pb=pieces) for p in powers]
        accs = [a + _mm(a, p, pa=pieces, pb=pieces) for a, p in zip(accs, powers)]
    return accs


def _pair_slice(j):
    return slice(j * LANES, (j + 1) * LANES)


def _sub_rows(tt):
    return [slice(s, s + SUB_T) for s in range(0, tt, SUB_T)]


def _stack_heads(x, halves):
    return jnp.concatenate([jnp.where(halves[0], x, 0.0), jnp.where(halves[1], x, 0.0)], axis=0)


class _Rwkv:
    def __init__(self, p, prm, vres, carry_ref, cst):
        mu, wa_up, w_bias, a_bias, g_up, k_k, k_a, r_k, self.ln_g, self.ln_b = prm
        self.cst = cst
        tt = p.shape[0]
        w = MIX_W
        rid = _iota((tt, 1), 0)
        xs = jnp.where(rid == 0, carry_ref[SUBLANES - 1:SUBLANES, :], pltpu.roll(p, 1, axis=0))
        carry_ref[...] = p[tt - SUBLANES:, :]
        xm = p + (xs - p) * mu
        r = xm[:, 0:w]
        k = xm[:, w:2 * w]
        v = xm[:, 2 * w:3 * w]
        x_lora = xm[:, 3 * w:3 * w + LANES]
        xg = xm[:, 3 * w + LANES:3 * w + 2 * LANES]
        z = jnp.where(cst.half[0], jnp.tanh(x_lora), x_lora)
        lwa = _mm(z.astype(BF16), wa_up)
        w_log = -_softplus(-(w_bias + lwa[:, :w])) - 0.5
        logw = -jnp.exp(w_log)
        a = _sigmoid(a_bias + lwa[:, w:])
        self.g = _mm(_sigmoid(xg).astype(BF16), g_up)
        self.v_first = v
        if vres is not None:
            vf, vw1, vw2, vb = vres
            lo = _mm(_mm(v.astype(BF16), vw1).astype(BF16), vw2)
            v = v + (vf - v) * _sigmoid(vb + lo)
        kk = k * k_k
        kk = kk * lax.rsqrt(_head_sum(kk * kk, cst) + L2_EPS)
        k2 = k * (1.0 + (a - 1.0) * k_a)
        bb = kk * a
        ct = _sum01_left(cst.cum_and_total, logw)
        cum = ct[:tt]
        tot = ct[tt:]
        inv_w = jnp.exp(-cum)
        rem = jnp.exp(tot - cum)
        self.wc = jnp.exp(tot)
        self.r_t = r * jnp.exp(cum)
        self.a_t = -kk * jnp.exp(cum - logw)
        self.b_t = bb * inv_w
        self.k_t = k2 * inv_w
        self.b_w = bb * rem
        self.k_w = k2 * rem
        self.v = v
        self.bonus_rk = r * k2 * r_k
        self.tt = tt

    def gram(self):
        cst, tt, st = self.cst, self.tt, SUB_T
        subs = _sub_rows(tt)
        self.m_rb, self.gv = [], []
        lmats = []
        for j in range(N_HEADS // 2):
            sl = _pair_slice(j)
            a_p, r_p, b_p, k_p = (t[:, sl] for t in (self.a_t, self.r_t, self.b_t, self.k_t))
            x2s = [_lowp(jnp.concatenate([b_p[rs], k_p[rs]], axis=0), P_TIME) for rs in subs]
            blocks = {}
            for half in range(2):
                m = cst.half[half]
                a_m, r_m = jnp.where(m, a_p, 0.0), jnp.where(m, r_p, 0.0)
                for si, (x2, rs) in enumerate(zip(x2s, subs)):
                    x1 = jnp.concatenate([a_m[rs], r_m[rs]], axis=0)
                    gram = _mm(x1, x2, _NT, pa=P_TIME, pb=P_TIME)
                    lmats.append(jnp.where(cst.sub_strict, gram[:st, :st], 0.0))
                    blocks[half, si] = (jnp.where(cst.sub_strict, gram[:st, st:], 0.0),
                                        jnp.where(cst.sub_incl, gram[st:, :st], 0.0),
                                        jnp.where(cst.sub_incl, gram[st:, st:], 0.0))
            for si, rs in enumerate(subs):
                (g_ak0, m_rb0, m_rk0), (g_ak1, m_rb1, m_rk1) = blocks[0, si], blocks[1, si]
                lhs = jnp.concatenate([jnp.concatenate([g_ak0, g_ak1], axis=1),
                                       jnp.concatenate([m_rk0, m_rk1], axis=1)], axis=0)
                self.gv.append(_mm(lhs, _stack_heads(self.v[rs, sl], cst.half), pa=P_APPLY, pb=P_APPLY))
                self.m_rb.append(_lowp(jnp.concatenate([m_rb0, m_rb1], axis=1), P_APPLY))
        return lmats

    def apply(self, t_invs):
        cst, tt, st = self.cst, self.tt, SUB_T
        subs = _sub_rows(tt)
        n_sub = len(subs)
        uts = []
        for j in range(N_HEADS // 2):
            for si, rs in enumerate(subs):
                t_pair = jnp.concatenate([t_invs[(2 * j) * n_sub + si], t_invs[(2 * j + 1) * n_sub + si]], axis=1)
                rhs = jnp.concatenate([self.gv[j * n_sub + si][:st], self.a_t[rs, _pair_slice(j)]], axis=1)
                uts.append(_mm(t_pair, _stack_heads(rhs, cst.half2), pa=P_APPLY, pb=P_APPLY))
        mms = [_mm(m_rb, _stack_heads(ut, cst.half2), pa=P_APPLY, pb=P_APPLY) for m_rb, ut in zip(self.m_rb, uts)]
        self.pairs = []
        for j in range(N_HEADS // 2):
            idx = range(j * n_sub, (j + 1) * n_sub)
            ut = jnp.concatenate([uts[i] for i in idx], axis=0)
            mm = jnp.concatenate([mms[i] for i in idx], axis=0)
            gv_bot = jnp.concatenate([self.gv[i][st:] for i in idx], axis=0)
            r_p = self.r_t[:, _pair_slice(j)]
            self.pairs.append((ut[:, :LANES], ut[:, LANES:], mm[:, :LANES] + gv_bot, r_p + mm[:, LANES:]))

    def transitions(self, j, c):
        cst = self.cst
        sl = _pair_slice(j)
        rc = slice(c * CHUNK, (c + 1) * CHUNK)
        u0, ta, _, _ = self.pairs[j]
        phi = jnp.where(cst.pair_eye, self.wc[c * CHUNK:c * CHUNK + 1, sl], 0.0) + jnp.where(
            cst.pair_bd, _mm(ta[rc], self.b_w[rc, sl], _TN, pa=P_STATE, pb=P_STATE), 0.0)
        delta = jnp.where(cst.pair_bd, _mm(jnp.concatenate([u0[rc], self.v[rc, sl]], axis=0),
                                           jnp.concatenate([self.b_w[rc, sl], self.k_w[rc, sl]], axis=0),
                                           _TN, pa=P_STATE, pb=P_STATE), 0.0)
        return phi, delta

    def step(self, j, c, s, trans, y_ref, y_off):
        rc = slice(c * CHUNK, (c + 1) * CHUNK)
        _, _, y0, ra = self.pairs[j]
        phi, delta = trans
        y_ref[rc, y_off + j * LANES:y_off + (j + 1) * LANES] = (
            _mm(ra[rc], s, _NT, pa=P_STATE, pb=P_STATE) + y0[rc])
        return _mm(s, phi, pa=P_STATE, pb=P_STATE) + delta

    def finish(self, y_ref, y_off):
        cst, w = self.cst, MIX_W
        y = y_ref[:, y_off:y_off + w]
        mean = _head_sum(y, cst) * (1.0 / HEAD_DIM)
        yc = y - mean
        var = _head_sum(yc * yc, cst) * (1.0 / HEAD_DIM)
        yn = yc * lax.rsqrt(var + RWKV_GN_EPS) * self.ln_g + self.ln_b
        bonus = _head_sum(self.bonus_rk, cst) * self.v
        y_ref[:, y_off:y_off + w] = (yn + bonus) * self.g


class _Gdn:
    def __init__(self, p_qkv, p_z, p_ab, prm, carry_ref, cst):
        conv_w, a_log, dt_bias, self.norm_g = prm
        self.cst = cst
        self.p_z = p_z
        tt = p_qkv.shape[0]
        w = MIX_W
        conv = _causal_conv(p_qkv, conv_w, carry_ref)
        qkv = conv * _sigmoid(conv)
        q = qkv[:, 0:w]
        k = qkv[:, w:2 * w]
        v = qkv[:, 2 * w:3 * w]
        ss = _head_sum(jnp.concatenate([q * q, k * k], axis=0), cst)
        q = q * lax.rsqrt(ss[:tt] + L2_EPS) * (HEAD_DIM ** -0.5)
        k = k * lax.rsqrt(ss[tt:] + L2_EPS)
        gb = jnp.where(cst.lane < N_HEADS, -jnp.exp(a_log) * _softplus(p_ab + dt_bias), _sigmoid(p_ab))
        beta = _sum01_right(gb, cst.expand_b)
        gt = _sum01_left(cst.cum_and_total, gb)
        self.gcs = gt[:tt]
        self.gcs_t = self.gcs.T
        gt_full = _sum01_right(gt, cst.expand_g)
        gc = gt_full[:tt]
        tot = gt_full[tt:]
        egc = jnp.exp(gc)
        self.q = q
        self.k = k
        self.kb = k * beta
        self.vb = v * beta
        self.kbe = self.kb * egc
        self.qd = q * egc
        self.kd = k * jnp.exp(tot - gc)
        self.gl = jnp.exp(tot)
        self.tt = tt

    def gram(self):
        cst, tt, st = self.cst, self.tt, SUB_T
        subs = _sub_rows(tt)
        self.qk = []
        lmats = []
        for j in range(N_HEADS // 2):
            sl = _pair_slice(j)
            kb_p, q_p = self.kb[:, sl], self.q[:, sl]
            k_lp = _lowp(self.k[:, sl], P_TIME)
            qk = {}
            for half in range(2):
                h = 2 * j + half
                m = cst.half[half]
                kb_m, q_m = jnp.where(m, kb_p, 0.0), jnp.where(m, q_p, 0.0)
                for si, rs in enumerate(subs):
                    diff = self.gcs[rs, h:h + 1] - self.gcs_t[h:h + 1, rs]
                    dec = jnp.where(cst.sub_incl, jnp.exp(jnp.where(cst.sub_incl, diff, 0.0)), 0.0)
                    x1 = jnp.concatenate([kb_m[rs], q_m[rs]], axis=0)
                    gram = _mm(x1, k_lp[rs], _NT, pa=P_TIME, pb=P_TIME)
                    a_mat = jnp.where(cst.sub_strict, gram[:st] * dec, 0.0)
                    lmats.append(-a_mat)
                    qk[half, si] = gram[st:] * dec
            for si in range(len(subs)):
                self.qk.append(_lowp(jnp.concatenate([qk[0, si], qk[1, si]], axis=1), P_APPLY))
        return lmats

    def apply(self, t_invs):
        cst = self.cst
        subs = _sub_rows(self.tt)
        n_sub = len(subs)
        uws = []
        for j in range(N_HEADS // 2):
            sl = _pair_slice(j)
            for si, rs in enumerate(subs):
                t_pair = jnp.concatenate([t_invs[(2 * j) * n_sub + si], t_invs[(2 * j + 1) * n_sub + si]], axis=1)
                rhs = jnp.concatenate([self.vb[rs, sl], self.kbe[rs, sl]], axis=1)
                uws.append(_mm(t_pair, _stack_heads(rhs, cst.half2), pa=P_APPLY, pb=P_APPLY))
        oqs = [_mm(qk, _stack_heads(uw, cst.half2), pa=P_APPLY, pb=P_APPLY) for qk, uw in zip(self.qk, uws)]
        self.pairs = []
        for j in range(N_HEADS // 2):
            idx = range(j * n_sub, (j + 1) * n_sub)
            uw = jnp.concatenate([uws[i] for i in idx], axis=0)
            oq = jnp.concatenate([oqs[i] for i in idx], axis=0)
            self.pairs.append((uw[:, :LANES], uw[:, LANES:], oq[:, :LANES], self.qd[:, _pair_slice(j)] - oq[:, LANES:]))

    def transitions(self, j, c):
        cst = self.cst
        sl = _pair_slice(j)
        rc = slice(c * CHUNK, (c + 1) * CHUNK)
        u, wm, _, _ = self.pairs[j]
        ku = _mm(self.kd[rc, sl], jnp.concatenate([u[rc], wm[rc]], axis=1), _TN, pa=P_STATE, pb=P_STATE)
        delta = jnp.where(cst.pair_bd, ku[:, :LANES], 0.0)
        phi = jnp.where(cst.pair_eye, self.gl[c * CHUNK:c * CHUNK + 1, sl], 0.0) - jnp.where(
            cst.pair_bd, ku[:, LANES:], 0.0)
        return phi, delta

    def step(self, j, c, s, trans, y_ref, y_off):
        rc = slice(c * CHUNK, (c + 1) * CHUNK)
        _, _, o0, qa = self.pairs[j]
        phi, delta = trans
        y_ref[rc, y_off + j * LANES:y_off + (j + 1) * LANES] = _mm(qa[rc], s, pa=P_STATE, pb=P_STATE) + o0[rc]
        return _mm(phi, s, pa=P_STATE, pb=P_STATE) + delta

    def finish(self, y_ref, y_off):
        cst, w = self.cst, MIX_W
        o = y_ref[:, y_off:y_off + w]
        on = o * lax.rsqrt(_head_sum(o * o, cst) * (1.0 / HEAD_DIM) + NORM_EPS) * self.norm_g
        y_ref[:, y_off:y_off + w] = on * (self.p_z * _sigmoid(self.p_z))


_N_LRU, _N_RWKV, _N_VRES, _N_GDN = 6, 10, 3, 4


def _mixer_body(*refs, has_vres):
    it = iter(refs)
    take = lambda n: [next(it) for _ in range(n)]
    x_ref, mixg_ref, win_ref, wout_ref = take(4)
    lru_prm = [r[...] for r in take(_N_LRU)]
    rwkv_prm = [r[...] for r in take(_N_RWKV)]
    vres = None
    if has_vres:
        vf_ref, = take(1)
        vres = [vf_ref[...]] + [r[...] for r in take(_N_VRES)]
    gdn_prm = [r[...] for r in take(_N_GDN)]
    o_ref, = take(1)
    vout_ref = None
    if not has_vres:
        vout_ref, = take(1)
    lru_carry, lru_h, rw_carry, rw_s, gd_carry, gd_s, y_scr = take(7)

    t_idx = pl.program_id(1)

    @pl.when(t_idx == 0)
    def _():
        for ref in (lru_carry, lru_h, rw_carry, rw_s, gd_carry, gd_s):
            ref[...] = jnp.zeros(ref.shape, ref.dtype)

    cst = _Consts()
    x = x_ref[...]
    h = _rms(x, mixg_ref[...]).astype(BF16)
    proj = _mm(h, win_ref[...])
    o = 0
    p_lx = proj[:, o:o + LRU_W]; o += LRU_W
    p_ly = proj[:, o:o + LRU_W]; o += LRU_W
    rw_in = 3 * MIX_W + 2 * LANES
    p_rw = proj[:, o:o + rw_in]; o += rw_in
    p_qkv = proj[:, o:o + 3 * MIX_W]; o += 3 * MIX_W
    p_z = proj[:, o:o + MIX_W]; o += MIX_W
    p_ab = proj[:, o:o + LANES]
    rw_off, gd_off = LRU_W, LRU_W + MIX_W

    _lru(p_lx, p_ly, lru_prm, lru_carry, lru_h, t_idx, y_scr)
    rw = _Rwkv(p_rw, rwkv_prm, vres, rw_carry, cst)
    if vout_ref is not None:
        vout_ref[...] = rw.v_first
    gd = _Gdn(p_qkv, p_z, p_ab, gdn_prm, gd_carry, cst)
    rw_l = rw.gram()
    t_invs = _nil_inverse_many(rw_l + gd.gram(), cst.sub_eye, P_INV)
    rw.apply(t_invs[:len(rw_l)])
    gd.apply(t_invs[len(rw_l):])
    n_pairs = N_HEADS // 2
    groups = ((rw, rw_s, rw_off), (gd, gd_s, gd_off))
    states = [[s_ref[j] for j in range(n_pairs)] for _, s_ref, _ in groups]
    for c in range(TILE_T // CHUNK):
        trans = [[grp.transitions(j, c) for j in range(n_pairs)] for grp, _, _ in groups]
        for gi, (grp, _, off) in enumerate(groups):
            for j in range(n_pairs):
                states[gi][j] = grp.step(j, c, states[gi][j], trans[gi][j], y_scr, off)
    for gi, (_, s_ref, _) in enumerate(groups):
        for j in range(n_pairs):
            s_ref[j] = states[gi][j]
    rw.finish(y_scr, rw_off)
    gd.finish(y_scr, gd_off)
    o_ref[...] = x + _mm(y_scr[...].astype(BF16), wout_ref[...])


def _mixer_call(x2, n_t, big, layer, lru_prm, rwkv_prm, vres, gdn_prm):
    n, d = x2.shape
    mixg, win, wout = big
    has_vres = vres is not None
    row_map = lambda b, t: (b * n_t + t, 0)
    args = [x2, mixg, win, wout]
    specs = [pl.BlockSpec((TILE_T, d), row_map), _const_spec(mixg.shape), _layer_spec(win.shape, layer),
             _layer_spec(wout.shape, layer)]

    def add_consts(arrs):
        for a in arrs:
            args.append(a)
            specs.append(_const_spec(a.shape))

    add_consts(lru_prm)
    add_consts(rwkv_prm)
    if has_vres:
        args.append(vres[0])
        specs.append(pl.BlockSpec((TILE_T, MIX_W), row_map))
        add_consts(vres[1:])
    add_consts(gdn_prm)
    out_shape = [jax.ShapeDtypeStruct((n, d), F32)]
    out_specs = [pl.BlockSpec((TILE_T, d), row_map)]
    if not has_vres:
        out_shape.append(jax.ShapeDtypeStruct((n, MIX_W), F32))
        out_specs.append(pl.BlockSpec((TILE_T, MIX_W), row_map))
    rw_in = 3 * MIX_W + 2 * LANES
    scratch = [
        pltpu.VMEM((SUBLANES, LRU_W), F32), pltpu.VMEM((SUBLANES, LRU_W), F32),
        pltpu.VMEM((SUBLANES, rw_in), F32), pltpu.VMEM((N_HEADS // 2, LANES, LANES), F32),
        pltpu.VMEM((SUBLANES, 3 * MIX_W), F32), pltpu.VMEM((N_HEADS // 2, LANES, LANES), F32),
        pltpu.VMEM((TILE_T, d), F32),
    ]
    res = pl.pallas_call(
        functools.partial(_mixer_body, has_vres=has_vres),
        grid=(n // (n_t * TILE_T), n_t),
        in_specs=specs,
        out_specs=out_specs,
        out_shape=out_shape,
        scratch_shapes=scratch,
        compiler_params=pltpu.CompilerParams(dimension_semantics=("arbitrary", "arbitrary"),
                                             vmem_limit_bytes=VMEM_LIMIT),
        name="mixer_vres" if has_vres else "mixer",
    )(*args)
    return res[0], (vres[0] if has_vres else res[1])


def _row(v):
    return v.reshape(1, -1).astype(F32)


def _pad_lanes(v, width=LANES):
    v = v.reshape(1, -1).astype(F32)
    return jnp.pad(v, ((0, 0), (0, width - v.shape[1])))


def kernel(x, ffn1_norm, ffn1_wi, ffn1_wo, mix_norm, w_in, w_out, lru_conv_w, lru_conv_b, lru_gate_a_w, lru_gate_a_b, lru_gate_x_w, lru_gate_x_b, lru_lambda, lru_out_norm, rwkv_mu, rwkv_w_up, rwkv_w_bias, rwkv_a_up, rwkv_a_bias, rwkv_g_up, rwkv_k_k, rwkv_k_a, rwkv_r_k, rwkv_ln_g, rwkv_ln_b, rwkv_vres_w1, rwkv_vres_w2, rwkv_vres_b, gdn_conv_w, gdn_a_log, gdn_dt_bias, gdn_norm, ffn2_norm, ffn2_wi, ffn2_wo, final_norm):
    bsz, seq, d = x.shape
    depth = ffn1_norm.shape[0]
    assert seq % TILE_T == 0 and d == LRU_W + 2 * MIX_W
    n_t = seq // TILE_T
    x2 = x.reshape(bsz * seq, d)
    d_in = w_in.shape[-1]
    d_in_pad = -(-d_in // LANES) * LANES
    lora = rwkv_w_up.shape[1]
    assert 2 * lora == LANES and rwkv_a_up.shape[1] == lora and rwkv_g_up.shape[1] == LANES
    v_first = None
    wi1, wo1, wi2, wo2 = (t.astype(BF16) for t in (ffn1_wi, ffn1_wo, ffn2_wi, ffn2_wo))
    win = jnp.pad(w_in.astype(BF16), ((0, 0), (0, 0), (0, d_in_pad - d_in)))
    wout = w_out.astype(BF16)
    for l in range(depth):
        x2 = _ffn_call(x2, ffn1_norm[l], wi1, wo1, l)
        big = (_row(mix_norm[l]), win, wout)
        gate_w = jnp.concatenate([jax.scipy.linalg.block_diag(*lru_gate_a_w[l]),
                                  jax.scipy.linalg.block_diag(*lru_gate_x_w[l])], axis=1).astype(BF16)
        gate_b = jnp.concatenate([lru_gate_a_b[l].reshape(1, -1), lru_gate_x_b[l].reshape(1, -1)], axis=1)
        lru_prm = [lru_conv_w[l], _row(lru_conv_b[l]), gate_w, gate_b, _row(lru_lambda[l]), _row(lru_out_norm[l])]
        wa_up = jnp.zeros((LANES, 2 * MIX_W), F32)
        wa_up = wa_up.at[:lora, :MIX_W].set(rwkv_w_up[l]).at[lora:, MIX_W:].set(rwkv_a_up[l]).astype(BF16)
        rwkv_prm = [_row(rwkv_mu[l]), wa_up, _row(rwkv_w_bias[l]), _row(rwkv_a_bias[l]), rwkv_g_up[l].astype(BF16),
                    _row(rwkv_k_k[l]), _row(rwkv_k_a[l]), _row(rwkv_r_k[l]), _row(rwkv_ln_g[l]), _row(rwkv_ln_b[l])]
        vres = None
        if l > 0:
            mv = rwkv_vres_w1.shape[-1]
            vw1 = jnp.pad(rwkv_vres_w1[l - 1], ((0, 0), (0, LANES - mv))).astype(BF16)
            vw2 = jnp.pad(rwkv_vres_w2[l - 1], ((0, LANES - mv), (0, 0))).astype(BF16)
            vres = [v_first, vw1, vw2, _row(rwkv_vres_b[l - 1])]
        gdn_prm = [gdn_conv_w[l], _pad_lanes(gdn_a_log[l]), _pad_lanes(gdn_dt_bias[l]),
                   _row(jnp.tile(gdn_norm[l], N_HEADS))]
        x2, v_first = _mixer_call(x2, n_t, big, l, lru_prm, rwkv_prm, vres, gdn_prm)
        last = l == depth - 1
        x2 = _ffn_call(x2, ffn2_norm[l], wi2, wo2, l, gf=final_norm if last else None)
    return x2.reshape(bsz, seq, d)
```
